```python
import jax, jax.numpy as jnp
from jax import lax
import numpy as np

D_MODEL = 4096
BATCH = 2
SEQ = 4096
DEPTH = 1
DEC_BATCH = 32
DEC_SEQ = 4
PAST_LEN = 8192
PAGE_SIZE = 128

N_BRANCHES = 3
RET_HEADS = 8
RET_DK = D_MODEL // 32
RET_DV = 2 * RET_DK
RET_CHUNK = 128
DIL_GROUPS = ((128, 1), (512, 4), (2048, 16))
DIL_HPG = 4
DIL_HEADS = DIL_HPG * len(DIL_GROUPS)
DIL_HD = D_MODEL // 32
DIL_BLOCK = 128
ALIBI_MAX_EXP = 8.0
MEM_LEN = 256
MEM_HEADS = 4
MEM_HD = 3 * D_MODEL // 32
PEER_HEADS = 8
PEER_NKEYS = 128
PEER_EXPERTS = PEER_NKEYS * PEER_NKEYS
PEER_TOPK = 16
PEER_DQ = D_MODEL // 16
PEER_BLOCK = 128
NORM_EPS = 1e-6
PROJ_WIDTH = (2 * RET_HEADS * RET_DK + 2 * RET_HEADS * RET_DV + 3 * DIL_HEADS * DIL_HD
              + MEM_HEADS * MEM_HD + N_BRANCHES * D_MODEL)

kernel_name = 'hybrid_retention_dilated_peer_step'


def rms_norm(x, w):
    xf = x.astype(jnp.float32)
    y = xf * lax.rsqrt(jnp.mean(xf * xf, axis=-1, keepdims=True) + NORM_EPS)
    return (y * w.astype(jnp.float32)).astype(x.dtype)


def split_projection(proj):
    sizes = (RET_HEADS * RET_DK, RET_HEADS * RET_DK, RET_HEADS * RET_DV, RET_HEADS * RET_DV,
             DIL_HEADS * DIL_HD, DIL_HEADS * DIL_HD, DIL_HEADS * DIL_HD,
             MEM_HEADS * MEM_HD, N_BRANCHES * D_MODEL)
    offs = np.cumsum(sizes)[:-1].tolist()
    return jnp.split(proj, offs, axis=-1)


def mixer_inputs(x, norm_w, w_in):
    n, s, _ = x.shape
    rq, rk, rv, rg, dq, dk, dv, mq, gates = split_projection(rms_norm(x, norm_w) @ w_in)
    rq = rq.reshape(n, s, RET_HEADS, RET_DK)
    rk = rk.reshape(n, s, RET_HEADS, RET_DK) * (RET_DK ** -0.5)
    rv = rv.reshape(n, s, RET_HEADS, RET_DV)
    dq = dq.reshape(n, s, DIL_HEADS, DIL_HD)
    dk = dk.reshape(n, s, DIL_HEADS, DIL_HD)
    dv = dv.reshape(n, s, DIL_HEADS, DIL_HD)
    mq = mq.reshape(n, s, MEM_HEADS, MEM_HD)
    return rq, rk, rv, rg, dq, dk, dv, mq, gates


def retention_log_decay():
    return jnp.log(1.0 - 2.0 ** (-5.0 - jnp.arange(RET_HEADS, dtype=jnp.float32)))


def retention_chunk(state, q, k, v):
    L = q.shape[1]
    lg = retention_log_decay()
    n = jnp.arange(L, dtype=jnp.float32)
    diff = n[:, None] - n[None, :]
    decay = jnp.where(diff[None] >= 0, jnp.exp(lg[:, None, None] * jnp.maximum(diff, 0.0)[None]), 0.0)
    inner = jnp.einsum('nlhd,nmhd->nhlm', q, k) * decay[None]
    o_inner = jnp.einsum('nhlm,nmhv->nlhv', inner, v)
    q_dec = jnp.exp(lg[None, :] * (n[:, None] + 1.0))
    o_cross = jnp.einsum('nlhd,nhdv->nlhv', q, state) * q_dec[None, :, :, None]
    k_dec = jnp.exp(lg[None, :] * (L - 1.0 - n[:, None]))
    new_state = (state * jnp.exp(lg * L)[None, :, None, None]
                 + jnp.einsum('nlhd,nlhv->nhdv', k * k_dec[None, :, :, None], v))
    return o_inner + o_cross, new_state


def retention_scan(state, q, k, v):
    n, s = q.shape[:2]
    nc = s // RET_CHUNK

    def to_chunks(a):
        return jnp.moveaxis(a.reshape(n, nc, RET_CHUNK, *a.shape[2:]), 1, 0)

    def step(st, qkv):
        o, st = retention_chunk(st, *qkv)
        return st, o

    st, o = lax.scan(step, state, (to_chunks(q), to_chunks(k), to_chunks(v)))
    return jnp.moveaxis(o, 0, 1).reshape(n, s, RET_HEADS, RET_DV), st


def group_norm_heads(o, w):
    of = o.astype(jnp.float32)
    mu = jnp.mean(of, axis=-1, keepdims=True)
    var = jnp.mean(jnp.square(of - mu), axis=-1, keepdims=True)
    y = (of - mu) * lax.rsqrt(var + NORM_EPS)
    return y.reshape(*o.shape[:2], -1) * w.astype(jnp.float32)


def alibi_slopes():
    return 2.0 ** (-ALIBI_MAX_EXP * (jnp.arange(DIL_HEADS, dtype=jnp.float32) + 1.0) / DIL_HEADS)


def dilated_group_prompt(q, k, v, win_steps, dil, slopes):
    B, S, Hg, hd = q.shape
    L = S // dil
    nb = -(-L // DIL_BLOCK)
    Lp = nb * DIL_BLOCK

    def by_residue(a):
        a = a.reshape(B, L, dil, Hg, hd).transpose(0, 2, 1, 3, 4)
        a = jnp.pad(a, ((0, 0), (0, 0), (0, Lp - L), (0, 0), (0, 0)))
        return a.reshape(B, dil, nb, DIL_BLOCK, Hg, hd)

    def with_prev(a):
        prev = jnp.pad(a, ((0, 0), (0, 0), (1, 0), (0, 0), (0, 0), (0, 0)))[:, :, :-1]
        return jnp.concatenate([prev, a], axis=3)

    qb = by_residue(q)
    kk = with_prev(by_residue(k))
    vv = with_prev(by_residue(v))
    qi = jnp.arange(DIL_BLOCK)[:, None]
    kj = jnp.arange(2 * DIL_BLOCK)[None, :]
    steps = qi + DIL_BLOCK - kj
    key_idx = jnp.arange(nb)[:, None, None] * DIL_BLOCK - DIL_BLOCK + kj[None]
    mask = ((steps >= 0) & (steps <= win_steps))[None] & (key_idx >= 0)
    bias = -slopes[:, None, None] * (steps * dil).astype(jnp.float32)[None]
    s = jnp.einsum('brnqhd,brnkhd->brnhqk', qb, kk).astype(jnp.float32) * (hd ** -0.5) + bias[None, None, None]
    s = jnp.where(mask[None, None, :, None], s, -jnp.inf)
    lse = jax.nn.logsumexp(s, axis=-1)
    p = jnp.exp(s - lse[..., None])
    o = jnp.einsum('brnhqk,brnkhd->brnqhd', p.astype(v.dtype), vv)
    o = o.reshape(B, dil, Lp, Hg, hd)[:, :, :L].transpose(0, 2, 1, 3, 4).reshape(B, S, Hg, hd)
    lse = lse.transpose(0, 1, 2, 4, 3).reshape(B, dil, Lp, Hg)[:, :, :L].transpose(0, 2, 1, 3).reshape(B, S, Hg)
    return o, lse


def dilated_group_sample(q, k_all, v_all, buf_len, win_steps, dil, slopes):
    T, hd = q.shape[1], q.shape[3]
    j = jnp.arange(win_steps + 1)
    idx = buf_len + jnp.arange(T)[:, None] - j[None, :] * dil
    valid = idx >= 0
    idx_c = jnp.maximum(idx, 0)
    kg = k_all[:, idx_c]
    vg = v_all[:, idx_c]
    s = (jnp.einsum('nthd,ntjhd->nthj', q, kg).astype(jnp.float32) * (hd ** -0.5)
         - slopes[None, None, :, None] * (j * dil).astype(jnp.float32)[None, None, None, :])
    s = jnp.where(valid[None, :, None, :], s, -jnp.inf)
    lse = jax.nn.logsumexp(s, axis=-1)
    p = jnp.exp(s - lse[..., None])
    o = jnp.einsum('nthj,ntjhd->nthd', p.astype(v_all.dtype), vg)
    return o, lse


def combine_groups(outs, lses):
    o = jnp.stack(outs, axis=0).astype(jnp.float32)
    a = jax.nn.softmax(jnp.stack(lses, axis=0), axis=0)
    return jnp.sum(a[..., None] * o, axis=0)


def memory_kv(mem, norm_w, w_kv):
    n = mem.shape[0]
    mk, mv = jnp.split(rms_norm(mem, norm_w) @ w_kv, 2, axis=-1)
    return (mk.reshape(n, MEM_LEN, MEM_HEADS, MEM_HD), mv.reshape(n, MEM_LEN, MEM_HEADS, MEM_HD))


def memory_attend(q, mk, mv):
    s = jnp.einsum('nlhd,nmhd->nhlm', q, mk).astype(jnp.float32) * (MEM_HD ** -0.5)
    p = jax.nn.softmax(s, axis=-1)
    return jnp.einsum('nhlm,nmhd->nlhd', p.astype(mv.dtype), mv)


def mixer_output(x, ret_o, rg, dil_o, mem_o, gates, ret_gn_w, w_ret_o, w_dil_o, w_mem_o, w_out):
    n, s, _ = x.shape
    r = (group_norm_heads(ret_o, ret_gn_w) * jax.nn.silu(rg.astype(jnp.float32))).astype(x.dtype) @ w_ret_o
    d = dil_o.reshape(n, s, DIL_HPG * DIL_HD).astype(x.dtype) @ w_dil_o
    m = mem_o.reshape(n, s, MEM_HEADS * MEM_HD).astype(x.dtype) @ w_mem_o
    g_r, g_d, g_m = jnp.split(jax.nn.sigmoid(gates.astype(jnp.float32)).astype(x.dtype), N_BRANCHES, axis=-1)
    return x + (g_r * r + g_d * d + g_m * m) @ w_out


def peer_block(xb, w_q, sub_keys, u_tab, v_tab):
    T = xb.shape[0]
    q = (xb @ w_q).reshape(T, PEER_HEADS, 2, PEER_DQ // 2)
    s = jnp.einsum('thcd,hcnd->thcn', q, sub_keys).astype(jnp.float32)
    top_s, top_i = lax.top_k(s, PEER_TOPK)
    cand_s = (top_s[:, :, 0, :, None] + top_s[:, :, 1, None, :]).reshape(T, PEER_HEADS, PEER_TOPK * PEER_TOPK)
    cand_i = (top_i[:, :, 0, :, None] * PEER_NKEYS + top_i[:, :, 1, None, :]).reshape(T, PEER_HEADS, PEER_TOPK * PEER_TOPK)
    best_s, best_pos = lax.top_k(cand_s, PEER_TOPK)
    expert = jnp.take_along_axis(cand_i, best_pos, axis=-1)
    g = jax.nn.softmax(best_s, axis=-1)
    u = u_tab[expert]
    v = v_tab[expert]
    act = jax.nn.gelu(jnp.einsum('thkd,td->thk', u, xb).astype(jnp.float32))
    return jnp.einsum('thk,thkd->td', (g * act).astype(v.dtype), v)


def peer_ffn(h, w_q, sub_keys, u_tab, v_tab):
    n, s, d = h.shape
    T = n * s
    nb = -(-T // PEER_BLOCK)
    hf = jnp.pad(h.reshape(T, d), ((0, nb * PEER_BLOCK - T), (0, 0))).reshape(nb, PEER_BLOCK, d)
    out = lax.map(lambda xb: peer_block(xb, w_q, sub_keys, u_tab, v_tab), hf)
    return out.reshape(nb * PEER_BLOCK, d)[:T].reshape(n, s, d)


def channel_mixer(x, norm_w, w_q, sub_keys, u_tab, v_tab):
    return x + peer_ffn(rms_norm(x, norm_w), w_q, sub_keys, u_tab, v_tab).astype(x.dtype)


def setup_inputs(seed: int = 0) -> dict:
    key = jax.random.key(seed)
    ks = iter(jax.random.split(key, 40))

    def nrm(shape, scale):
        return jax.random.normal(next(ks), shape, jnp.float32) * scale

    def gain(shape):
        return 1.0 + 0.01 * jax.random.normal(next(ks), shape, jnp.float32)

    win_lens = [min(w, PAST_LEN) for (w, _) in DIL_GROUPS]
    wshape = lambda g: (DEPTH, DEC_BATCH, win_lens[g], DIL_HPG, DIL_HD)
    mshape = (DEPTH, DEC_BATCH, MEM_LEN, MEM_HEADS, MEM_HD)
    return {
        'x_prompt': nrm((BATCH, SEQ, D_MODEL), 1.0),
        'x_sample': nrm((DEC_BATCH, DEC_SEQ, D_MODEL), 1.0),
        'mem_prompt': nrm((BATCH, MEM_LEN, D_MODEL), 1.0),
        'cache_ret_state': nrm((DEPTH, DEC_BATCH, RET_HEADS, RET_DK, RET_DV), 1.0),
        'cache_win_k1': nrm(wshape(0), 1.0),
        'cache_win_v1': nrm(wshape(0), 1.0),
        'cache_win_k2': nrm(wshape(1), 1.0),
        'cache_win_v2': nrm(wshape(1), 1.0),
        'cache_win_k3': nrm(wshape(2), 1.0),
        'cache_win_v3': nrm(wshape(2), 1.0),
        'cache_mem_k': nrm(mshape, 1.0),
        'cache_mem_v': nrm(mshape, 1.0),
        'norm1_w': gain((DEPTH, D_MODEL)),
        'w_in': nrm((DEPTH, D_MODEL, PROJ_WIDTH), D_MODEL ** -0.5),
        'ret_gn_w': gain((DEPTH, RET_HEADS * RET_DV)),
        'w_ret_o': nrm((DEPTH, RET_HEADS * RET_DV, D_MODEL), (RET_HEADS * RET_DV) ** -0.5),
        'w_dil_o': nrm((DEPTH, DIL_HPG * DIL_HD, D_MODEL), (DIL_HPG * DIL_HD) ** -0.5),
        'mem_norm_w': gain((DEPTH, D_MODEL)),
        'w_mem_kv': nrm((DEPTH, D_MODEL, 2 * MEM_HEADS * MEM_HD), D_MODEL ** -0.5),
        'w_mem_o': nrm((DEPTH, MEM_HEADS * MEM_HD, D_MODEL), (MEM_HEADS * MEM_HD) ** -0.5),
        'w_out': nrm((DEPTH, D_MODEL, D_MODEL), D_MODEL ** -0.5),
        'norm2_w': gain((DEPTH, D_MODEL)),
        'w_peer_q': nrm((DEPTH, D_MODEL, PEER_HEADS * PEER_DQ), D_MODEL ** -0.5),
        'peer_sub_keys': nrm((DEPTH, PEER_HEADS, 2, PEER_NKEYS, PEER_DQ // 2), (PEER_DQ // 2) ** -0.5),
        'peer_u': nrm((DEPTH, PEER_EXPERTS, D_MODEL), D_MODEL ** -0.5),
        'peer_v': nrm((DEPTH, PEER_EXPERTS, D_MODEL), PEER_HEADS ** -0.5),
        'final_norm_w': gain((D_MODEL,)),
    }


def reference(x_prompt, x_sample, mem_prompt, cache_ret_state, cache_win_k1, cache_win_v1,
              cache_win_k2, cache_win_v2, cache_win_k3, cache_win_v3, cache_mem_k, cache_mem_v,
              norm1_w, w_in, ret_gn_w, w_ret_o, w_dil_o, mem_norm_w, w_mem_kv, w_mem_o, w_out,
              norm2_w, w_peer_q, peer_sub_keys, peer_u, peer_v, final_norm_w):
    win_k = (cache_win_k1, cache_win_k2, cache_win_k3)
    win_v = (cache_win_v1, cache_win_v2, cache_win_v3)
    slopes = alibi_slopes()
    hp, hs = x_prompt, x_sample
    prompt_states, sample_states = [], []
    for l in range(DEPTH):
        rq, rk, rv, rg, dq, dk, dv, mq, gates = mixer_inputs(hp, norm1_w[l], w_in[l])
        n_p, s_p = hp.shape[:2]
        ret_o, ret_st = retention_scan(jnp.zeros((n_p, RET_HEADS, RET_DK, RET_DV), jnp.float32), rq, rk, rv)
        outs, lses, bufs = [], [], []
        for g, (win, dil) in enumerate(DIL_GROUPS):
            sl = slice(g * DIL_HPG, (g + 1) * DIL_HPG)
            o, lse = dilated_group_prompt(dq[:, :, sl], dk[:, :, sl], dv[:, :, sl], win // dil, dil, slopes[sl])
            outs.append(o)
            lses.append(lse)
            keep = min(win, s_p)
            bufs += [dk[:, s_p - keep:, sl], dv[:, s_p - keep:, sl]]
        dil_o = combine_groups(outs, lses)
        mk, mv = memory_kv(mem_prompt, mem_norm_w[l], w_mem_kv[l])
        mem_o = memory_attend(mq, mk, mv)
        hp = mixer_output(hp, ret_o, rg, dil_o, mem_o, gates, ret_gn_w[l], w_ret_o[l], w_dil_o[l], w_mem_o[l], w_out[l])
        hp = channel_mixer(hp, norm2_w[l], w_peer_q[l], peer_sub_keys[l], peer_u[l], peer_v[l])
        prompt_states.append((ret_st, *bufs, mk, mv))

        rq, rk, rv, rg, dq, dk, dv, mq, gates = mixer_inputs(hs, norm1_w[l], w_in[l])
        t_s = hs.shape[1]
        ret_o_s, ret_st_s = retention_chunk(cache_ret_state[l].astype(jnp.float32), rq, rk, rv)
        outs, lses, bufs_s = [], [], []
        for g, (win, dil) in enumerate(DIL_GROUPS):
            sl = slice(g * DIL_HPG, (g + 1) * DIL_HPG)
            buf_len = win_k[g].shape[2]
            k_all = jnp.concatenate([win_k[g][l].astype(dk.dtype), dk[:, :, sl]], axis=1)
            v_all = jnp.concatenate([win_v[g][l].astype(dv.dtype), dv[:, :, sl]], axis=1)
            o, lse = dilated_group_sample(dq[:, :, sl], k_all, v_all, buf_len, win // dil, dil, slopes[sl])
            outs.append(o)
            lses.append(lse)
            keep = min(win, buf_len + t_s)
            bufs_s += [k_all[:, k_all.shape[1] - keep:], v_all[:, v_all.shape[1] - keep:]]
        dil_o = combine_groups(outs, lses)
        mem_o = memory_attend(mq, cache_mem_k[l].astype(mq.dtype), cache_mem_v[l].astype(mq.dtype))
        hs = mixer_output(hs, ret_o_s, rg, dil_o, mem_o, gates, ret_gn_w[l], w_ret_o[l], w_dil_o[l], w_mem_o[l], w_out[l])
        hs = channel_mixer(hs, norm2_w[l], w_peer_q[l], peer_sub_keys[l], peer_u[l], peer_v[l])
        sample_states.append((ret_st_s, *bufs_s))

    y_prompt = rms_norm(hp, final_norm_w)
    y_sample = rms_norm(hs, final_norm_w)
    ret_p, k1_p, v1_p, k2_p, v2_p, k3_p, v3_p, mk_p, mv_p = [jnp.stack(s, axis=0) for s in zip(*prompt_states)]
    ret_s, k1_s, v1_s, k2_s, v2_s, k3_s, v3_s = [jnp.stack(s, axis=0) for s in zip(*sample_states)]
    return (y_prompt, y_sample, ret_p, k1_p, v1_p, k2_p, v2_p, k3_p, v3_p, mk_p, mv_p,
            ret_s, k1_s, v1_s, k2_s, v2_s, k3_s, v3_s)
```

```python
import functools
import math

import jax
import jax.numpy as jnp
from jax import lax
from jax.experimental import pallas as pl
from jax.experimental.pallas import tpu as pltpu

F32, BF16, I32 = jnp.float32, jnp.bfloat16, jnp.int32

D_MODEL = 4096
RET_HEADS, RET_DK, RET_DV, RET_CHUNK = 8, 128, 256, 128
DIL_GROUPS = ((128, 1), (512, 4), (2048, 16))
DIL_HPG, DIL_HD, DIL_BLOCK = 4, 128, 128
DIL_HEADS = DIL_HPG * len(DIL_GROUPS)
ALIBI_MAX_EXP = 8.0
MEM_HEADS, MEM_HD = 4, 384
PEER_HEADS, PEER_NKEYS, PEER_TOPK, PEER_DQ = 8, 128, 16, 256
NORM_EPS = 1e-6

RET_QK = RET_HEADS * RET_DK
RET_V = RET_HEADS * RET_DV
DIL_W = DIL_HPG * DIL_HD
DIL_ALL = DIL_HEADS * DIL_HD
MEM_W = MEM_HEADS * MEM_HD
OFF_RQ = 0
OFF_RK = OFF_RQ + RET_QK
OFF_RV = OFF_RK + RET_QK
OFF_RG = OFF_RV + RET_V
OFF_DQ = OFF_RG + RET_V
OFF_DK = OFF_DQ + DIL_ALL
OFF_DV = OFF_DK + DIL_ALL
OFF_MQ = OFF_DV + DIL_ALL
OFF_GATE = OFF_MQ + MEM_W
PROJ_W = OFF_GATE + 3 * D_MODEL

SAMPLE_PAD = 8
V7X_VMEM_BYTES = 64 * 1024 * 1024
MIB = 1024 * 1024

_NT = (((1,), (1,)), ((), ()))


def _pick(n, cands):
    for c in cands:
        if n % c == 0:
            return c
    raise ValueError(f"no tile in {cands} divides {n}")


def _params(vmem_mib):
    return pltpu.CompilerParams(vmem_limit_bytes=min(vmem_mib * MIB, V7X_VMEM_BYTES - 2 * MIB))


def _rmsnorm_kernel(x_ref, w_ref, o_ref):
    x = x_ref[...]
    ms = jnp.mean(x * x, axis=-1, keepdims=True)
    o_ref[...] = (x * lax.rsqrt(ms + NORM_EPS) * w_ref[...]).astype(o_ref.dtype)


def _rmsnorm(x, w, out_dtype):
    m, d = x.shape
    tm = _pick(m, (320, 256, 128, 64, 8))
    return pl.pallas_call(
        _rmsnorm_kernel, grid=(m // tm,),
        in_specs=[pl.BlockSpec((tm, d), lambda i: (i, 0)), pl.BlockSpec((1, d), lambda i: (0, 0))],
        out_specs=pl.BlockSpec((tm, d), lambda i: (i, 0)),
        out_shape=jax.ShapeDtypeStruct((m, d), out_dtype),
        compiler_params=_params(48), name="rmsnorm")(x, w.reshape(1, d))


def _add_rmsnorm_kernel(a_ref, b_ref, w_ref, o_ref):
    x = a_ref[...] + b_ref[...]
    ms = jnp.mean(x * x, axis=-1, keepdims=True)
    o_ref[...] = x * lax.rsqrt(ms + NORM_EPS) * w_ref[...]


def _add_rmsnorm(a, b, w, row0, rows):
    d = a.shape[1]
    tm = _pick(math.gcd(row0, rows) if row0 else rows, (256, 128, 64, 8))
    blk0 = row0 // tm
    spec = pl.BlockSpec((tm, d), lambda i: (blk0 + i, 0))
    return pl.pallas_call(
        _add_rmsnorm_kernel, grid=(rows // tm,),
        in_specs=[spec, spec, pl.BlockSpec((1, d), lambda i: (0, 0))],
        out_specs=pl.BlockSpec((tm, d), lambda i: (i, 0)),
        out_shape=jax.ShapeDtypeStruct((rows, d), F32),
        compiler_params=_params(56), name="add_rmsnorm")(a, b, w.reshape(1, d))


def _add_kernel(a_ref, b_ref, o_ref):
    o_ref[...] = a_ref[...] + b_ref[...]


def _add(a, b):
    m, d = a.shape
    tm = _pick(m, (320, 256, 128, 64, 8))
    spec = pl.BlockSpec((tm, d), lambda i: (i, 0))
    return pl.pallas_call(_add_kernel, grid=(m // tm,), in_specs=[spec, spec], out_specs=spec,
                          out_shape=jax.ShapeDtypeStruct((m, d), F32),
                          compiler_params=_params(56), name="residual_add")(a, b)


def _mm_kernel(a_ref, b_ref, o_ref):
    o_ref[...] = jnp.dot(a_ref[...], b_ref[...], preferred_element_type=F32).astype(o_ref.dtype)


def _mm_res_kernel(a_ref, b_ref, r_ref, o_ref):
    o_ref[...] = r_ref[...] + jnp.dot(a_ref[...], b_ref[...], preferred_element_type=F32)


def _matmul(a, b, res=None, tm_cands=(832, 640, 512, 416, 320, 256, 128, 64, 8), tn=512, name="matmul"):
    m, k = a.shape
    n = b.shape[1]
    tm = _pick(m, tm_cands)
    tn = _pick(n, (tn, 256, 128))
    in_specs = [pl.BlockSpec((tm, k), lambda i, j: (i, 0)), pl.BlockSpec((k, tn), lambda i, j: (0, j))]
    args = [a, b]
    body = _mm_kernel
    if res is not None:
        in_specs.append(pl.BlockSpec((tm, tn), lambda i, j: (i, j)))
        args.append(res)
        body = _mm_res_kernel
    return pl.pallas_call(
        body, grid=(m // tm, n // tn), in_specs=in_specs,
        out_specs=pl.BlockSpec((tm, tn), lambda i, j: (i, j)),
        out_shape=jax.ShapeDtypeStruct((m, n), F32),
        compiler_params=_params(56), name=name)(*args)


def _ret_log_decay(h):
    return math.log(1.0 - 2.0 ** (-5.0 - h))


def _group_norm_gate(o, gnw, rg):
    mu = jnp.mean(o, axis=-1, keepdims=True)
    d = o - mu
    var = jnp.mean(d * d, axis=-1, keepdims=True)
    y = d * lax.rsqrt(var + NORM_EPS) * gnw
    return (y * (rg * jax.nn.sigmoid(rg))).astype(BF16)


def _ret_prompt_kernel(q_ref, k_ref, v_ref, rg_ref, gnw_ref, ar_ref, st_ref, state_scr):
    c = pl.program_id(1)
    L = RET_CHUNK

    @pl.when(c == 0)
    def _():
        state_scr[...] = jnp.zeros_like(state_scr)

    diff = (lax.broadcasted_iota(I32, (L, L), 0) - lax.broadcasted_iota(I32, (L, L), 1)).astype(F32)
    row_k = lax.broadcasted_iota(I32, (L, RET_DK), 0).astype(F32)
    row_v = lax.broadcasted_iota(I32, (L, RET_DV), 0).astype(F32)
    for h in range(RET_HEADS):
        lg = _ret_log_decay(h)
        ck = slice(h * RET_DK, (h + 1) * RET_DK)
        cv = slice(h * RET_DV, (h + 1) * RET_DV)
        decay = jnp.where(diff >= 0, jnp.exp(lg * jnp.maximum(diff, 0.0)), 0.0)
        q = q_ref[:, ck].astype(BF16)
        k = k_ref[:, ck] * (RET_DK ** -0.5)
        v = v_ref[:, cv].astype(BF16)
        s = lax.dot_general(q, k.astype(BF16), _NT, preferred_element_type=F32)
        st = state_scr[h]
        o = jnp.dot((s * decay).astype(BF16), v, preferred_element_type=F32)
        o = o + jnp.dot(q, st.astype(BF16), preferred_element_type=F32) * jnp.exp(lg * (row_v + 1.0))
        kd = (k * jnp.exp(lg * (L - 1.0 - row_k))).T.astype(BF16)
        state_scr[h] = st * math.exp(lg * L) + jnp.dot(kd, v, preferred_element_type=F32)
        ar_ref[:, cv] = _group_norm_gate(o, gnw_ref[:, cv], rg_ref[:, cv])

    @pl.when(c == pl.num_programs(1) - 1)
    def _():
        st_ref[0] = state_scr[...]


def _retention_prompt(proj, gnw, batch, seq):
    nc = seq // RET_CHUNK
    row = lambda n, c: n * nc + c
    return pl.pallas_call(
        _ret_prompt_kernel, grid=(batch, nc),
        in_specs=[pl.BlockSpec((RET_CHUNK, RET_QK), lambda n, c: (row(n, c), OFF_RQ // RET_QK)),
                  pl.BlockSpec((RET_CHUNK, RET_QK), lambda n, c: (row(n, c), OFF_RK // RET_QK)),
                  pl.BlockSpec((RET_CHUNK, RET_V), lambda n, c: (row(n, c), OFF_RV // RET_V)),
                  pl.BlockSpec((RET_CHUNK, RET_V), lambda n, c: (row(n, c), OFF_RG // RET_V)),
                  pl.BlockSpec((1, RET_V), lambda n, c: (0, 0))],
        out_specs=[pl.BlockSpec((RET_CHUNK, RET_V), lambda n, c: (row(n, c), 0)),
                   pl.BlockSpec((1, RET_HEADS, RET_DK, RET_DV), lambda n, c: (n, 0, 0, 0))],
        out_shape=[jax.ShapeDtypeStruct((batch * seq, RET_V), BF16),
                   jax.ShapeDtypeStruct((batch, RET_HEADS, RET_DK, RET_DV), F32)],
        scratch_shapes=[pltpu.VMEM((RET_HEADS, RET_DK, RET_DV), F32)],
        compiler_params=_params(32), name="retention_prompt")(proj, proj, proj, proj, gnw.reshape(1, RET_V))


def _ret_sample_kernel(q_ref, k_ref, v_ref, rg_ref, gnw_ref, st_in_ref, ar_ref, st_out_ref, o_scr, *, ts, ns_blk):
    i = pl.program_id(0)
    R = q_ref.shape[0]

    def sample_and_pos(shape, axis):
        r = lax.broadcasted_iota(I32, shape, axis).astype(F32)
        sid = jnp.floor((r + 0.5) * (1.0 / ts))
        return sid.astype(I32), r - sid * ts

    sid_r, t_r = sample_and_pos((R, R), 0)
    sid_c, t_c = sample_and_pos((R, R), 1)
    sid_k, t_k = sample_and_pos((R, RET_DK), 0)
    _, t_v = sample_and_pos((R, RET_DV), 0)
    for h in range(RET_HEADS):
        lg = _ret_log_decay(h)
        ck = slice(h * RET_DK, (h + 1) * RET_DK)
        cv = slice(h * RET_DV, (h + 1) * RET_DV)
        q = q_ref[:, ck]
        k = k_ref[:, ck] * (RET_DK ** -0.5)
        v = v_ref[:, cv].astype(BF16)

        @pl.when(i == 0)
        def _():
            s = lax.dot_general(q.astype(BF16), k.astype(BF16), _NT, preferred_element_type=F32)
            dt = t_r - t_c
            decay = jnp.where((sid_r == sid_c) & (dt >= 0), jnp.exp(lg * jnp.maximum(dt, 0.0)), 0.0)
            o_scr[:, cv] = jnp.dot((s * decay).astype(BF16), v, preferred_element_type=F32)

        q_dec = jnp.exp(lg * (t_v + 1.0))
        kd = k * jnp.exp(lg * (ts - 1.0 - t_k))
        for j in range(ns_blk):
            mine = sid_k == i * ns_blk + j
            st = st_in_ref[j, h]
            qm = jnp.where(mine, q, 0.0).astype(BF16)
            o_scr[:, cv] += jnp.dot(qm, st.astype(BF16), preferred_element_type=F32) * q_dec
            kdm = jnp.where(mine, kd, 0.0).T.astype(BF16)
            st_out_ref[j, h] = st * math.exp(lg * ts) + jnp.dot(kdm, v, preferred_element_type=F32)

    @pl.when(i == pl.num_programs(0) - 1)
    def _():
        for h in range(RET_HEADS):
            cv = slice(h * RET_DV, (h + 1) * RET_DV)
            ar_ref[:, cv] = _group_norm_gate(o_scr[:, cv], gnw_ref[:, cv], rg_ref[:, cv])


def _retention_sample(proj_s, gnw, state, ts):
    rows = proj_s.shape[0]
    ns = state.shape[0]
    ns_blk = _pick(ns, (4, 2, 1))
    return pl.pallas_call(
        functools.partial(_ret_sample_kernel, ts=ts, ns_blk=ns_blk), grid=(ns // ns_blk,),
        in_specs=[pl.BlockSpec((rows, RET_QK), lambda i: (0, OFF_RQ // RET_QK)),
                  pl.BlockSpec((rows, RET_QK), lambda i: (0, OFF_RK // RET_QK)),
                  pl.BlockSpec((rows, RET_V), lambda i: (0, OFF_RV // RET_V)),
                  pl.BlockSpec((rows, RET_V), lambda i: (0, OFF_RG // RET_V)),
                  pl.BlockSpec((1, RET_V), lambda i: (0, 0)),
                  pl.BlockSpec((ns_blk, RET_HEADS, RET_DK, RET_DV), lambda i: (i, 0, 0, 0))],
        out_specs=[pl.BlockSpec((rows, RET_V), lambda i: (0, 0)),
                   pl.BlockSpec((ns_blk, RET_HEADS, RET_DK, RET_DV), lambda i: (i, 0, 0, 0))],
        out_shape=[jax.ShapeDtypeStruct((rows, RET_V), BF16),
                   jax.ShapeDtypeStruct(state.shape, F32)],
        scratch_shapes=[pltpu.VMEM((rows, RET_V), F32)],
        compiler_params=_params(40), name="retention_sample")(
            proj_s, proj_s, proj_s, proj_s, gnw.reshape(1, RET_V), state)


def _alibi_slope(head):
    return 2.0 ** (-ALIBI_MAX_EXP * (head + 1.0) / DIL_HEADS)


def _dil_prompt_kernel(q_ref, k_ref, kp_ref, v_ref, vp_ref, o_ref, lse_ref, *, group, dil, win_steps):
    blk = pl.program_id(2)
    B = DIL_BLOCK
    qi = lax.broadcasted_iota(I32, (B, B), 0)
    kj = lax.broadcasted_iota(I32, (B, B), 1)
    steps_prev = qi + B - kj
    steps_cur = qi - kj
    ok_prev = (steps_prev <= win_steps) & (jnp.full((B, B), blk, I32) > 0)
    ok_cur = (steps_cur >= 0) & (steps_cur <= win_steps)
    scale = DIL_HD ** -0.5
    for hh in range(DIL_HPG):
        slope = _alibi_slope(group * DIL_HPG + hh)
        c = slice(hh * DIL_HD, (hh + 1) * DIL_HD)
        q = q_ref[:, c].astype(BF16)
        sp = lax.dot_general(q, kp_ref[:, c].astype(BF16), _NT, preferred_element_type=F32)
        sc = lax.dot_general(q, k_ref[:, c].astype(BF16), _NT, preferred_element_type=F32)
        sp = jnp.where(ok_prev, sp * scale - slope * (steps_prev * dil).astype(F32), -jnp.inf)
        sc = jnp.where(ok_cur, sc * scale - slope * (steps_cur * dil).astype(F32), -jnp.inf)
        m = jnp.maximum(jnp.max(sp, axis=-1, keepdims=True), jnp.max(sc, axis=-1, keepdims=True))
        ep = jnp.exp(sp - m)
        ec = jnp.exp(sc - m)
        l = jnp.sum(ep, axis=-1, keepdims=True) + jnp.sum(ec, axis=-1, keepdims=True)
        o = jnp.dot((ep / l).astype(BF16), vp_ref[:, c].astype(BF16), preferred_element_type=F32)
        o = o + jnp.dot((ec / l).astype(BF16), v_ref[:, c].astype(BF16), preferred_element_type=F32)
        o_ref[:, c] = o
        lse_ref[:, c] = jnp.broadcast_to(m + jnp.log(l), (B, DIL_HD))


def _dilated_prompt(proj, group, batch, seq):
    win, dil = DIL_GROUPS[group]
    t_all = proj.shape[0]
    assert t_all % dil == 0 and (seq // dil) % DIL_BLOCK == 0
    nblk = (seq // dil) // DIL_BLOCK
    view = proj.reshape(t_all // dil, dil * PROJ_W)
    band = PROJ_W // DIL_W

    def spec(off, prev):
        col0 = off // DIL_W + group
        if prev:
            return pl.BlockSpec((DIL_BLOCK, DIL_W), lambda b, r, j: (b * nblk + jnp.maximum(j - 1, 0), r * band + col0))
        return pl.BlockSpec((DIL_BLOCK, DIL_W), lambda b, r, j: (b * nblk + j, r * band + col0))

    out_spec = pl.BlockSpec((DIL_BLOCK, DIL_W), lambda b, r, j: (b * nblk + j, r))
    rows = batch * seq // dil
    o, lse = pl.pallas_call(
        functools.partial(_dil_prompt_kernel, group=group, dil=dil, win_steps=win // dil),
        grid=(batch, dil, nblk),
        in_specs=[spec(OFF_DQ, False), spec(OFF_DK, False), spec(OFF_DK, True), spec(OFF_DV, False), spec(OFF_DV, True)],
        out_specs=[out_spec, out_spec],
        out_shape=[jax.ShapeDtypeStruct((rows, dil * DIL_W), F32)] * 2,
        compiler_params=_params(32), name=f"dilated_prompt_g{group}")(view, view, view, view, view)
    return o.reshape(batch * seq, DIL_W), lse.reshape(batch * seq, DIL_W)


def _dil_sample_kernel(q_ref, kn_ref, vn_ref, kc_ref, vc_ref, o_ref, lse_ref, ok_ref, ov_ref,
                       *, group, dil, win_steps, ts):
    buf = kc_ref.shape[0]
    P = q_ref.shape[0]
    t = lax.broadcasted_iota(I32, (P, buf), 0)
    r = lax.broadcasted_iota(I32, (P, buf), 1)
    diff = buf + t - r
    span = win_steps * dil
    ok = ((diff & (dil - 1)) == 0) & (diff <= span) & (diff >= 0)
    bias_steps = diff.astype(F32)
    t1 = lax.broadcasted_iota(I32, (P, 1), 0)
    scale = DIL_HD ** -0.5
    for hh in range(DIL_HPG):
        slope = _alibi_slope(group * DIL_HPG + hh)
        c = slice(hh * DIL_HD, (hh + 1) * DIL_HD)
        q = q_ref[:, c]
        s = lax.dot_general(q.astype(BF16), kc_ref[:, c].astype(BF16), _NT, preferred_element_type=F32)
        s = jnp.where(ok, s * scale - slope * bias_steps, -jnp.inf)
        m = jnp.max(s, axis=-1, keepdims=True)
        s_new = []
        for rp in range(ts):
            d = t1 - rp
            ok_n = (d >= 0) & ((d & (dil - 1)) == 0) & (d <= span)
            sn = jnp.sum(q * kn_ref[rp:rp + 1, c], axis=-1, keepdims=True) * scale - slope * d.astype(F32)
            sn = jnp.where(ok_n, sn, -jnp.inf)
            s_new.append(sn)
            m = jnp.maximum(m, sn)
        e = jnp.exp(s - m)
        e_new = [jnp.exp(sn - m) for sn in s_new]
        l = jnp.sum(e, axis=-1, keepdims=True)
        for en in e_new:
            l = l + en
        o = jnp.dot((e / l).astype(BF16), vc_ref[:, c].astype(BF16), preferred_element_type=F32)
        for rp in range(ts):
            o = o + (e_new[rp] / l) * vn_ref[rp:rp + 1, c]
        o_ref[:, c] = o
        lse_ref[:, c] = jnp.broadcast_to(m + jnp.log(l), (P, DIL_HD))
    ok_ref[0:buf - ts, :] = kc_ref[ts:buf, :]
    ok_ref[buf - ts:buf, :] = kn_ref[0:ts, :]
    ov_ref[0:buf - ts, :] = vc_ref[ts:buf, :]
    ov_ref[buf - ts:buf, :] = vn_ref[0:ts, :]


def _dilated_sample(q8, kn8, vn8, cache_k, cache_v, group, ts):
    win, dil = DIL_GROUPS[group]
    ns, buf = cache_k.shape[0], cache_k.shape[1]
    assert buf == win and buf % dil == 0, "cached window must hold exactly one full window"
    small = pl.BlockSpec((None, SAMPLE_PAD, DIL_W), lambda n: (n, 0, 0))
    big = pl.BlockSpec((None, buf, DIL_W), lambda n: (n, 0, 0))
    return pl.pallas_call(
        functools.partial(_dil_sample_kernel, group=group, dil=dil, win_steps=win // dil, ts=ts),
        grid=(ns,), in_specs=[small, small, small, big, big],
        out_specs=[small, small, big, big],
        out_shape=[jax.ShapeDtypeStruct((ns, SAMPLE_PAD, DIL_W), F32)] * 2
        + [jax.ShapeDtypeStruct((ns, buf, DIL_W), F32)] * 2,
        compiler_params=_params(48), name=f"dilated_sample_g{group}")(q8, kn8, vn8, cache_k, cache_v)


def _combine_kernel(o1_ref, o2_ref, o3_ref, l1_ref, l2_ref, l3_ref, out_ref):
    l1, l2, l3 = l1_ref[...], l2_ref[...], l3_ref[...]
    m = jnp.maximum(jnp.maximum(l1, l2), l3)
    e1, e2, e3 = jnp.exp(l1 - m), jnp.exp(l2 - m), jnp.exp(l3 - m)
    z = e1 + e2 + e3
    out = (e1 / z) * o1_ref[...] + (e2 / z) * o2_ref[...] + (e3 / z) * o3_ref[...]
    out_ref[...] = out.astype(out_ref.dtype)


def _combine_groups(outs, lses):
    m, w = outs[0].shape
    tm = _pick(m, (1024, 512, 256, 128, 64, 8))
    spec = pl.BlockSpec((tm, w), lambda i: (i, 0))
    return pl.pallas_call(_combine_kernel, grid=(m // tm,), in_specs=[spec] * 6, out_specs=spec,
                          out_shape=jax.ShapeDtypeStruct((m, w), BF16),
                          compiler_params=_params(40), name="combine_groups")(*outs, *lses)


def _mem_attn_kernel(q_ref, k_ref, v_ref, o_ref):
    scale = MEM_HD ** -0.5
    for h in range(MEM_HEADS):
        c = slice(h * MEM_HD, (h + 1) * MEM_HD)
        s = lax.dot_general(q_ref[:, c].astype(BF16), k_ref[:, c].astype(BF16), _NT,
                            preferred_element_type=F32) * scale
        e = jnp.exp(s - jnp.max(s, axis=-1, keepdims=True))
        p = e / jnp.sum(e, axis=-1, keepdims=True)
        o_ref[:, c] = jnp.dot(p.astype(BF16), v_ref[:, c].astype(BF16),
                              preferred_element_type=F32).astype(o_ref.dtype)


def _memory_attend_prompt(proj, mkv, batch, seq):
    mem_len = mkv.shape[1]
    tq = _pick(seq, (512, 256, 128))
    nq = seq // tq
    return pl.pallas_call(
        _mem_attn_kernel, grid=(batch, nq),
        in_specs=[pl.BlockSpec((tq, MEM_W), lambda b, i: (b * nq + i, OFF_MQ // MEM_W)),
                  pl.BlockSpec((None, mem_len, MEM_W), lambda b, i: (b, 0, 0)),
                  pl.BlockSpec((None, mem_len, MEM_W), lambda b, i: (b, 0, 1))],
        out_specs=pl.BlockSpec((tq, MEM_W), lambda b, i: (b * nq + i, 0)),
        out_shape=jax.ShapeDtypeStruct((batch * seq, MEM_W), BF16),
        compiler_params=_params(40), name="memory_attend_prompt")(proj, mkv, mkv)


def _memory_attend_sample(q8, mem_k, mem_v):
    ns, mem_len = mem_k.shape[0], mem_k.shape[1]
    small = pl.BlockSpec((None, SAMPLE_PAD, MEM_W), lambda n: (n, 0, 0))
    big = pl.BlockSpec((None, mem_len, MEM_W), lambda n: (n, 0, 0))
    return pl.pallas_call(
        _mem_attn_kernel, grid=(ns,), in_specs=[small, big, big], out_specs=small,
        out_shape=jax.ShapeDtypeStruct((ns, SAMPLE_PAD, MEM_W), F32),
        compiler_params=_params(24), name="memory_attend_sample")(q8, mem_k, mem_v)


def _mix_kernel(ar_ref, ad_ref, am_ref, wr_ref, wd_ref, wm_ref, gr_ref, gd_ref, gm_ref, o_ref):
    r = jnp.dot(ar_ref[...], wr_ref[...], preferred_element_type=F32)
    d = jnp.dot(ad_ref[...], wd_ref[...], preferred_element_type=F32)
    m = jnp.dot(am_ref[...], wm_ref[...], preferred_element_type=F32)
    mix = jax.nn.sigmoid(gr_ref[...]) * r + jax.nn.sigmoid(gd_ref[...]) * d + jax.nn.sigmoid(gm_ref[...]) * m
    o_ref[...] = mix.astype(o_ref.dtype)


def _branch_mix(a_r, a_d, a_m, w_r, w_d, w_m, proj):
    t = a_r.shape[0]
    tm = _pick(t, (832, 640, 512, 416, 320, 256, 128, 64, 8))
    tn = 512
    nj = D_MODEL // tn
    lhs = lambda w: pl.BlockSpec((tm, w), lambda i, j: (i, 0))
    rhs = lambda w: pl.BlockSpec((w, tn), lambda i, j: (0, j))
    gate = lambda b: pl.BlockSpec((tm, tn), lambda i, j: (i, OFF_GATE // tn + b * nj + j))
    return pl.pallas_call(
        _mix_kernel, grid=(t // tm, nj),
        in_specs=[lhs(RET_V), lhs(DIL_W), lhs(MEM_W), rhs(RET_V), rhs(DIL_W), rhs(MEM_W), gate(0), gate(1), gate(2)],
        out_specs=pl.BlockSpec((tm, tn), lambda i, j: (i, j)),
        out_shape=jax.ShapeDtypeStruct((t, D_MODEL), BF16),
        compiler_params=_params(56), name="branch_mix")(a_r, a_d, a_m, w_r, w_d, w_m, proj, proj, proj)


def _peer_topk_kernel(q_ref, keys_ref, i_ref, j_ref, g_ref):
    tk = q_ref.shape[0]
    K = PEER_TOPK
    iota_k = lax.broadcasted_iota(I32, (K, tk), 0)

    def top_k_rows(s, nrows):
        iota = lax.broadcasted_iota(I32, (nrows, tk), 0)

        def body(k, carry):
            s, top_s, top_i = carry
            m = jnp.max(s, axis=0, keepdims=True)
            pos = jnp.min(jnp.where(s == m, iota, nrows), axis=0, keepdims=True)
            s = jnp.where(iota == pos, -jnp.inf, s)
            sel = iota_k == k
            return s, jnp.where(sel, m, top_s), jnp.where(sel, pos, top_i)

        _, top_s, top_i = lax.fori_loop(0, K, body, (s, jnp.zeros((K, tk), F32), jnp.zeros((K, tk), I32)))
        return top_s, top_i

    halves = []
    for c in range(2):
        qc = q_ref[:, c * PEER_NKEYS:(c + 1) * PEER_NKEYS].astype(BF16)
        s_t = lax.dot_general(keys_ref[0, c].astype(BF16), qc, _NT, preferred_element_type=F32)
        halves.append(top_k_rows(s_t, PEER_NKEYS))
    (s1, i1), (s2, i2) = halves
    cand = jnp.concatenate([s1[a:a + 1, :] + s2 for a in range(K)], axis=0)
    best_s, best_p = top_k_rows(cand, K * K)
    pa = jnp.right_shift(best_p, K.bit_length() - 1)
    pb = best_p - pa * K
    ei = jnp.zeros((K, tk), I32)
    ej = jnp.zeros((K, tk), I32)
    for a in range(K):
        ei = jnp.where(pa == a, i1[a:a + 1, :], ei)
        ej = jnp.where(pb == a, i2[a:a + 1, :], ej)
    e = jnp.exp(best_s - jnp.max(best_s, axis=0, keepdims=True))
    i_ref[...] = ei.astype(F32)
    j_ref[...] = ej.astype(F32)
    g_ref[...] = e / jnp.sum(e, axis=0, keepdims=True)


def _peer_route(q, sub_keys):
    t = q.shape[0]
    tk = _pick(t, (256, 128))
    out_spec = pl.BlockSpec((PEER_TOPK, tk), lambda i, h: (h, i))
    return pl.pallas_call(
        _peer_topk_kernel, grid=(t // tk, PEER_HEADS),
        in_specs=[pl.BlockSpec((tk, PEER_DQ), lambda i, h: (i, h)),
                  pl.BlockSpec((1, 2, PEER_NKEYS, PEER_DQ // 2), lambda i, h: (h, 0, 0, 0))],
        out_specs=[out_spec] * 3,
        out_shape=[jax.ShapeDtypeStruct((PEER_HEADS * PEER_TOPK, t), F32)] * 3,
        compiler_params=_params(24), name="peer_route")(q, sub_keys)


def _peer_wbuild_kernel(i_ref, j_ref, g_ref, w_ref, it_scr, jt_scr, gt_scr):
    tw = w_ref.shape[0]
    it_scr[...] = i_ref[...].T
    jt_scr[...] = j_ref[...].T
    gt_scr[...] = g_ref[...].T
    n = PEER_NKEYS
    iota_r = lax.broadcasted_iota(I32, (n, i_ref.shape[0]), 0).astype(F32)

    def body(t, carry):
        irow = it_scr[pl.ds(t, 1), :]
        jrow = jt_scr[pl.ds(t, 1), :]
        grow = gt_scr[pl.ds(t, 1), :]
        a_t = jnp.where(iota_r == irow, grow, 0.0).astype(BF16)
        b_t = jnp.where(iota_r == jrow, 1.0, 0.0).astype(BF16)
        w = lax.dot_general(a_t, b_t, _NT, preferred_element_type=F32)
        w_ref[t] = w.astype(BF16)
        return carry

    lax.fori_loop(0, tw, body, 0)


def _peer_wbuild(ei, ej, g):
    nsel, t = ei.shape
    tw = 128
    in_spec = pl.BlockSpec((nsel, tw), lambda i: (0, i))
    n = PEER_NKEYS
    w = pl.pallas_call(
        _peer_wbuild_kernel, grid=(t // tw,), in_specs=[in_spec] * 3,
        out_specs=pl.BlockSpec((tw, n, n), lambda i: (i, 0, 0)),
        out_shape=jax.ShapeDtypeStruct((t, n, n), BF16),
        scratch_shapes=[pltpu.VMEM((tw, nsel), F32)] * 3,
        compiler_params=_params(24), name="peer_wbuild")(ei, ej, g)
    return w.reshape(t, n * n)


def _gelu_tanh(x):
    return x * (0.5 * (1.0 + jnp.tanh(math.sqrt(2.0 / math.pi) * (x + 0.044715 * (x * x * x)))))


def _peer_mlp_kernel(x_ref, u_ref, v_ref, w_ref, o_ref):
    e = pl.program_id(1)
    act = lax.dot_general(x_ref[...], u_ref[...], _NT, preferred_element_type=F32)
    p = (w_ref[...].astype(F32) * _gelu_tanh(act)).astype(BF16)
    part = jnp.dot(p, v_ref[...], preferred_element_type=F32)

    @pl.when(e == 0)
    def _():
        o_ref[...] = part

    @pl.when(e > 0)
    def _():
        o_ref[...] += part


def _peer_mlp(n2, u, v, w):
    t, d = n2.shape
    ne = u.shape[0]
    tm = _pick(t, (416, 320, 256, 128, 64, 8))
    eb = 512
    return pl.pallas_call(
        _peer_mlp_kernel, grid=(t // tm, ne // eb),
        in_specs=[pl.BlockSpec((tm, d), lambda i, e: (i, 0)),
                  pl.BlockSpec((eb, d), lambda i, e: (e, 0)),
                  pl.BlockSpec((eb, d), lambda i, e: (e, 0)),
                  pl.BlockSpec((tm, eb), lambda i, e: (i, e))],
        out_specs=pl.BlockSpec((tm, d), lambda i, e: (i, 0)),
        out_shape=jax.ShapeDtypeStruct((t, d), F32),
        compiler_params=_params(60), name="peer_mlp")(n2, u, v, w)


def _pad_sample_rows(a, ns, ts):
    a = a.reshape(ns, ts, a.shape[-1])
    return jnp.pad(a, ((0, 0), (0, SAMPLE_PAD - ts), (0, 0)))


def kernel(x_prompt, x_sample, mem_prompt, cache_ret_state, cache_win_k1, cache_win_v1, cache_win_k2, cache_win_v2, cache_win_k3, cache_win_v3, cache_mem_k, cache_mem_v, norm1_w, w_in, ret_gn_w, w_ret_o, w_dil_o, mem_norm_w, w_mem_kv, w_mem_o, w_out, norm2_w, w_peer_q, peer_sub_keys, peer_u, peer_v, final_norm_w):
    batch, seq, d = x_prompt.shape
    ns, ts, _ = x_sample.shape
    mem_len = mem_prompt.shape[1]
    depth = norm1_w.shape[0]
    assert d == D_MODEL and seq % RET_CHUNK == 0 and ts <= SAMPLE_PAD
    tp, tsamp = batch * seq, ns * ts
    win_k = (cache_win_k1, cache_win_k2, cache_win_k3)
    win_v = (cache_win_v1, cache_win_v2, cache_win_v3)

    x_all = jnp.concatenate([x_prompt.reshape(tp, d), x_sample.reshape(tsamp, d)], axis=0)
    prompt_states, sample_states = [], []
    for l in range(depth):
        proj = _matmul(_rmsnorm(x_all, norm1_w[l], BF16), w_in[l].astype(BF16),
                       tm_cands=(1664, 832, 640, 512, 416, 320, 256, 128, 64, 8), name="input_projection")
        proj_s = proj[tp:]

        ar_p, ret_p = _retention_prompt(proj, ret_gn_w[l], batch, seq)
        ar_s, ret_s = _retention_sample(proj_s, ret_gn_w[l], cache_ret_state[l], ts)

        outs_p, lses_p, outs_s, lses_s, bufs_p, bufs_s = [], [], [], [], [], []
        proj_p3 = proj[:tp].reshape(batch, seq, PROJ_W)
        for g, (win, dil) in enumerate(DIL_GROUPS):
            o, lse = _dilated_prompt(proj, g, batch, seq)
            outs_p.append(o)
            lses_p.append(lse)
            keep = min(win, seq)
            for off in (OFF_DK, OFF_DV):
                c0 = off + g * DIL_W
                bufs_p.append(proj_p3[:, seq - keep:, c0:c0 + DIL_W].reshape(batch, keep, DIL_HPG, DIL_HD))
            q8, kn8, vn8 = (_pad_sample_rows(proj_s[:, off + g * DIL_W: off + (g + 1) * DIL_W], ns, ts)
                            for off in (OFF_DQ, OFF_DK, OFF_DV))
            buf = win_k[g].shape[2]
            o8, lse8, k_new, v_new = _dilated_sample(
                q8, kn8, vn8, win_k[g][l].reshape(ns, buf, DIL_W), win_v[g][l].reshape(ns, buf, DIL_W), g, ts)
            outs_s.append(o8[:, :ts].reshape(tsamp, DIL_W))
            lses_s.append(lse8[:, :ts].reshape(tsamp, DIL_W))
            bufs_s += [k_new.reshape(ns, buf, DIL_HPG, DIL_HD), v_new.reshape(ns, buf, DIL_HPG, DIL_HD)]
        ad_p = _combine_groups(outs_p, lses_p)
        ad_s = _combine_groups(outs_s, lses_s)

        mem_n = _rmsnorm(mem_prompt.reshape(batch * mem_len, d), mem_norm_w[l], BF16)
        mkv = _matmul(mem_n, w_mem_kv[l].astype(BF16), name="memory_kv").reshape(batch, mem_len, 2 * MEM_W)
        am_p = _memory_attend_prompt(proj, mkv, batch, seq)
        mq8 = _pad_sample_rows(proj_s[:, OFF_MQ:OFF_MQ + MEM_W], ns, ts)
        am_s = _memory_attend_sample(mq8, cache_mem_k[l].reshape(ns, mem_len, MEM_W),
                                     cache_mem_v[l].reshape(ns, mem_len, MEM_W))
        am_s = am_s[:, :ts].reshape(tsamp, MEM_W).astype(BF16)

        a_r = jnp.concatenate([ar_p, ar_s], axis=0)
        a_d = jnp.concatenate([ad_p, ad_s], axis=0)
        a_m = jnp.concatenate([am_p, am_s], axis=0)
        mix = _branch_mix(a_r, a_d, a_m, w_ret_o[l].astype(BF16), w_dil_o[l].astype(BF16),
                          w_mem_o[l].astype(BF16), proj)
        h = _matmul(mix, w_out[l].astype(BF16), res=x_all, name="output_projection")

        n2 = _rmsnorm(h, norm2_w[l], BF16)
        pq = _matmul(n2, w_peer_q[l].astype(BF16), name="peer_query")
        ei, ej, gw = _peer_route(pq, peer_sub_keys[l])
        w_dense = _peer_wbuild(ei, ej, gw)
        peer = _peer_mlp(n2, peer_u[l].astype(BF16), peer_v[l].astype(BF16), w_dense)

        mk = mkv[:, :, :MEM_W].reshape(batch, mem_len, MEM_HEADS, MEM_HD)
        mv = mkv[:, :, MEM_W:].reshape(batch, mem_len, MEM_HEADS, MEM_HD)
        prompt_states.append((ret_p, *bufs_p, mk, mv))
        sample_states.append((ret_s, *bufs_s))
        if l + 1 < depth:
            x_all = _add(h, peer)

    y_prompt = _add_rmsnorm(h, peer, final_norm_w, 0, tp).reshape(batch, seq, d)
    y_sample = _add_rmsnorm(h, peer, final_norm_w, tp, tsamp).reshape(ns, ts, d)
    p_out = [jnp.stack(s, axis=0) for s in zip(*prompt_states)]
    s_out = [jnp.stack(s, axis=0) for s in zip(*sample_states)]
    return (y_prompt, y_sample, *p_out, *s_out)
```

```python
import functools
import math

import jax
import jax.numpy as jnp
from jax import lax
from jax.experimental import pallas as pl
from jax.experimental.pallas import tpu as pltpu

F32, BF16, I32 = jnp.float32, jnp.bfloat16, jnp.int32

D_MODEL = 4096
RET_HEADS, RET_DK, RET_DV, RET_CHUNK = 8, 128, 256, 128
DIL_GROUPS = ((128, 1), (512, 4), (2048, 16))
DIL_HPG, DIL_HD, DIL_BLOCK = 4, 128, 128
DIL_HEADS = DIL_HPG * len(DIL_GROUPS)
ALIBI_MAX_EXP = 8.0
MEM_HEADS, MEM_HD = 4, 384
PEER_HEADS, PEER_NKEYS, PEER_TOPK, PEER_DQ = 8, 128, 16, 256
NORM_EPS = 1e-6

RET_QK = RET_HEADS * RET_DK
RET_V = RET_HEADS * RET_DV
DIL_W = DIL_HPG * DIL_HD
DIL_ALL = DIL_HEADS * DIL_HD
MEM_W = MEM_HEADS * MEM_HD
OFF_RQ = 0
OFF_RK = OFF_RQ + RET_QK
OFF_RV = OFF_RK + RET_QK
OFF_RG = OFF_RV + RET_V
OFF_DQ = OFF_RG + RET_V
OFF_DK = OFF_DQ + DIL_ALL
OFF_DV = OFF_DK + DIL_ALL
OFF_MQ = OFF_DV + DIL_ALL
OFF_GATE = OFF_MQ + MEM_W
PROJ_W = OFF_GATE + 3 * D_MODEL

SAMPLE_PAD = 8
W_ROW_PITCH = PEER_NKEYS + 8
V7X_VMEM_BYTES = 64 * 1024 * 1024
MIB = 1024 * 1024

_NT = (((1,), (1,)), ((), ()))


def _pick(n, cands):
    for c in cands:
        if n % c == 0:
            return c
    raise ValueError(f"no tile in {cands} divides {n}")


def _params(vmem_mib):
    return pltpu.CompilerParams(vmem_limit_bytes=min(vmem_mib * MIB, V7X_VMEM_BYTES - 2 * MIB))


def _rmsnorm_kernel(x_ref, w_ref, o_ref):
    x = x_ref[...]
    ms = jnp.mean(x * x, axis=-1, keepdims=True)
    o_ref[...] = (x * lax.rsqrt(ms + NORM_EPS) * w_ref[...]).astype(o_ref.dtype)


def _rmsnorm(x, w, out_dtype):
    m, d = x.shape
    tm = _pick(m, (320, 256, 128, 64, 8))
    return pl.pallas_call(
        _rmsnorm_kernel, grid=(m // tm,),
        in_specs=[pl.BlockSpec((tm, d), lambda i: (i, 0)), pl.BlockSpec((1, d), lambda i: (0, 0))],
        out_specs=pl.BlockSpec((tm, d), lambda i: (i, 0)),
        out_shape=jax.ShapeDtypeStruct((m, d), out_dtype),
        compiler_params=_params(48), name="rmsnorm")(x, w.reshape(1, d))


def _add_rmsnorm_kernel(a_ref, b_ref, w_ref, o_ref):
    x = a_ref[...] + b_ref[...]
    ms = jnp.mean(x * x, axis=-1, keepdims=True)
    o_ref[...] = x * lax.rsqrt(ms + NORM_EPS) * w_ref[...]


def _add_rmsnorm(a, b, w, row0, rows):
    d = a.shape[1]
    tm = _pick(math.gcd(row0, rows) if row0 else rows, (256, 128, 64, 8))
    blk0 = row0 // tm
    spec = pl.BlockSpec((tm, d), lambda i: (blk0 + i, 0))
    return pl.pallas_call(
        _add_rmsnorm_kernel, grid=(rows // tm,),
        in_specs=[spec, spec, pl.BlockSpec((1, d), lambda i: (0, 0))],
        out_specs=pl.BlockSpec((tm, d), lambda i: (i, 0)),
        out_shape=jax.ShapeDtypeStruct((rows, d), F32),
        compiler_params=_params(56), name="add_rmsnorm")(a, b, w.reshape(1, d))


def _add_kernel(a_ref, b_ref, o_ref):
    o_ref[...] = a_ref[...] + b_ref[...]


def _add(a, b):
    m, d = a.shape
    tm = _pick(m, (320, 256, 128, 64, 8))
    spec = pl.BlockSpec((tm, d), lambda i: (i, 0))
    return pl.pallas_call(_add_kernel, grid=(m // tm,), in_specs=[spec, spec], out_specs=spec,
                          out_shape=jax.ShapeDtypeStruct((m, d), F32),
                          compiler_params=_params(56), name="residual_add")(a, b)


def _mm_kernel(a_ref, b_ref, o_ref):
    o_ref[...] = jnp.dot(a_ref[...], b_ref[...], preferred_element_type=F32).astype(o_ref.dtype)


def _mm_res_kernel(a_ref, b_ref, r_ref, o_ref):
    o_ref[...] = r_ref[...] + jnp.dot(a_ref[...], b_ref[...], preferred_element_type=F32)


def _matmul(a, b, res=None, tm_cands=(832, 640, 512, 416, 320, 256, 128, 64, 8), tn=512, name="matmul"):
    m, k = a.shape
    n = b.shape[1]
    tm = _pick(m, tm_cands)
    tn = _pick(n, (tn, 256, 128))
    in_specs = [pl.BlockSpec((tm, k), lambda i, j: (i, 0)), pl.BlockSpec((k, tn), lambda i, j: (0, j))]
    args = [a, b]
    body = _mm_kernel
    if res is not None:
        in_specs.append(pl.BlockSpec((tm, tn), lambda i, j: (i, j)))
        args.append(res)
        body = _mm_res_kernel
    return pl.pallas_call(
        body, grid=(m // tm, n // tn), in_specs=in_specs,
        out_specs=pl.BlockSpec((tm, tn), lambda i, j: (i, j)),
        out_shape=jax.ShapeDtypeStruct((m, n), F32),
        compiler_params=_params(56), name=name)(*args)


def _ret_log_decay(h):
    return math.log(1.0 - 2.0 ** (-5.0 - h))


def _group_norm_gate(o, gnw, rg):
    mu = jnp.mean(o, axis=-1, keepdims=True)
    d = o - mu
    var = jnp.mean(d * d, axis=-1, keepdims=True)
    y = d * lax.rsqrt(var + NORM_EPS) * gnw
    return (y * (rg * jax.nn.sigmoid(rg))).astype(BF16)


def _ret_prompt_kernel(q_ref, k_ref, v_ref, rg_ref, gnw_ref, ar_ref, st_ref, state_scr):
    c = pl.program_id(1)
    L = RET_CHUNK

    @pl.when(c == 0)
    def _():
        state_scr[...] = jnp.zeros_like(state_scr)

    diff = (lax.broadcasted_iota(I32, (L, L), 0) - lax.broadcasted_iota(I32, (L, L), 1)).astype(F32)
    row_k = lax.broadcasted_iota(I32, (L, RET_DK), 0).astype(F32)
    row_v = lax.broadcasted_iota(I32, (L, RET_DV), 0).astype(F32)
    for h in range(RET_HEADS):
        lg = _ret_log_decay(h)
        ck = slice(h * RET_DK, (h + 1) * RET_DK)
        cv = slice(h * RET_DV, (h + 1) * RET_DV)
        decay = jnp.where(diff >= 0, jnp.exp(lg * jnp.maximum(diff, 0.0)), 0.0)
        q = q_ref[:, ck].astype(BF16)
        k = k_ref[:, ck] * (RET_DK ** -0.5)
        v = v_ref[:, cv].astype(BF16)
        s = lax.dot_general(q, k.astype(BF16), _NT, preferred_element_type=F32)
        st = state_scr[h]
        o = jnp.dot((s * decay).astype(BF16), v, preferred_element_type=F32)
        o = o + jnp.dot(q, st.astype(BF16), preferred_element_type=F32) * jnp.exp(lg * (row_v + 1.0))
        kd = (k * jnp.exp(lg * (L - 1.0 - row_k))).T.astype(BF16)
        state_scr[h] = st * math.exp(lg * L) + jnp.dot(kd, v, preferred_element_type=F32)
        ar_ref[:, cv] = _group_norm_gate(o, gnw_ref[:, cv], rg_ref[:, cv])

    @pl.when(c == pl.num_programs(1) - 1)
    def _():
        st_ref[0] = state_scr[...]


def _retention_prompt(proj, gnw, batch, seq):
    nc = seq // RET_CHUNK
    row = lambda n, c: n * nc + c
    return pl.pallas_call(
        _ret_prompt_kernel, grid=(batch, nc),
        in_specs=[pl.BlockSpec((RET_CHUNK, RET_QK), lambda n, c: (row(n, c), OFF_RQ // RET_QK)),
                  pl.BlockSpec((RET_CHUNK, RET_QK), lambda n, c: (row(n, c), OFF_RK // RET_QK)),
                  pl.BlockSpec((RET_CHUNK, RET_V), lambda n, c: (row(n, c), OFF_RV // RET_V)),
                  pl.BlockSpec((RET_CHUNK, RET_V), lambda n, c: (row(n, c), OFF_RG // RET_V)),
                  pl.BlockSpec((1, RET_V), lambda n, c: (0, 0))],
        out_specs=[pl.BlockSpec((RET_CHUNK, RET_V), lambda n, c: (row(n, c), 0)),
                   pl.BlockSpec((1, RET_HEADS, RET_DK, RET_DV), lambda n, c: (n, 0, 0, 0))],
        out_shape=[jax.ShapeDtypeStruct((batch * seq, RET_V), BF16),
                   jax.ShapeDtypeStruct((batch, RET_HEADS, RET_DK, RET_DV), F32)],
        scratch_shapes=[pltpu.VMEM((RET_HEADS, RET_DK, RET_DV), F32)],
        compiler_params=_params(32), name="retention_prompt")(proj, proj, proj, proj, gnw.reshape(1, RET_V))


def _ret_sample_kernel(q_ref, k_ref, v_ref, rg_ref, gnw_ref, st_in_ref, ar_ref, st_out_ref, o_scr, *, ts, ns_blk):
    i = pl.program_id(0)
    R = q_ref.shape[0]

    def sample_and_pos(shape, axis):
        r = lax.broadcasted_iota(I32, shape, axis).astype(F32)
        sid = jnp.floor((r + 0.5) * (1.0 / ts))
        return sid.astype(I32), r - sid * ts

    sid_r, t_r = sample_and_pos((R, R), 0)
    sid_c, t_c = sample_and_pos((R, R), 1)
    sid_k, t_k = sample_and_pos((R, RET_DK), 0)
    _, t_v = sample_and_pos((R, RET_DV), 0)
    for h in range(RET_HEADS):
        lg = _ret_log_decay(h)
        ck = slice(h * RET_DK, (h + 1) * RET_DK)
        cv = slice(h * RET_DV, (h + 1) * RET_DV)
        q = q_ref[:, ck]
        k = k_ref[:, ck] * (RET_DK ** -0.5)
        v = v_ref[:, cv].astype(BF16)

        @pl.when(i == 0)
        def _():
            s = lax.dot_general(q.astype(BF16), k.astype(BF16), _NT, preferred_element_type=F32)
            dt = t_r - t_c
            decay = jnp.where((sid_r == sid_c) & (dt >= 0), jnp.exp(lg * jnp.maximum(dt, 0.0)), 0.0)
            o_scr[:, cv] = jnp.dot((s * decay).astype(BF16), v, preferred_element_type=F32)

        q_dec = jnp.exp(lg * (t_v + 1.0))
        kd = k * jnp.exp(lg * (ts - 1.0 - t_k))
        for j in range(ns_blk):
            mine = sid_k == i * ns_blk + j
            st = st_in_ref[j, h]
            qm = jnp.where(mine, q, 0.0).astype(BF16)
            o_scr[:, cv] += jnp.dot(qm, st.astype(BF16), preferred_element_type=F32) * q_dec
            kdm = jnp.where(mine, kd, 0.0).T.astype(BF16)
            st_out_ref[j, h] = st * math.exp(lg * ts) + jnp.dot(kdm, v, preferred_element_type=F32)

    @pl.when(i == pl.num_programs(0) - 1)
    def _():
        for h in range(RET_HEADS):
            cv = slice(h * RET_DV, (h + 1) * RET_DV)
            ar_ref[:, cv] = _group_norm_gate(o_scr[:, cv], gnw_ref[:, cv], rg_ref[:, cv])


def _retention_sample(proj_s, gnw, state, ts):
    rows = proj_s.shape[0]
    ns = state.shape[0]
    ns_blk = _pick(ns, (4, 2, 1))
    return pl.pallas_call(
        functools.partial(_ret_sample_kernel, ts=ts, ns_blk=ns_blk), grid=(ns // ns_blk,),
        in_specs=[pl.BlockSpec((rows, RET_QK), lambda i: (0, OFF_RQ // RET_QK)),
                  pl.BlockSpec((rows, RET_QK), lambda i: (0, OFF_RK // RET_QK)),
                  pl.BlockSpec((rows, RET_V), lambda i: (0, OFF_RV // RET_V)),
                  pl.BlockSpec((rows, RET_V), lambda i: (0, OFF_RG // RET_V)),
                  pl.BlockSpec((1, RET_V), lambda i: (0, 0)),
                  pl.BlockSpec((ns_blk, RET_HEADS, RET_DK, RET_DV), lambda i: (i, 0, 0, 0))],
        out_specs=[pl.BlockSpec((rows, RET_V), lambda i: (0, 0)),
                   pl.BlockSpec((ns_blk, RET_HEADS, RET_DK, RET_DV), lambda i: (i, 0, 0, 0))],
        out_shape=[jax.ShapeDtypeStruct((rows, RET_V), BF16),
                   jax.ShapeDtypeStruct(state.shape, F32)],
        scratch_shapes=[pltpu.VMEM((rows, RET_V), F32)],
        compiler_params=_params(40), name="retention_sample")(
            proj_s, proj_s, proj_s, proj_s, gnw.reshape(1, RET_V), state)


def _alibi_slope(head):
    return 2.0 ** (-ALIBI_MAX_EXP * (head + 1.0) / DIL_HEADS)


def _dil_prompt_kernel(slope_ref, q_ref, k_ref, kp_ref, v_ref, vp_ref, o_ref, lse_ref, *, group, dil, win_steps, nq):
    j = pl.program_id(1)
    hh = pl.program_id(2)
    slope = slope_ref[group * DIL_HPG + hh]
    B = DIL_BLOCK
    qi = lax.broadcasted_iota(I32, (B, B), 0)
    kj = lax.broadcasted_iota(I32, (B, B), 1)
    steps_prev = qi + B - kj
    steps_cur = qi - kj
    in_prev = steps_prev <= win_steps
    in_prev_first = in_prev & (jnp.full((B, B), j, I32) > 0)
    in_cur = (steps_cur >= 0) & (steps_cur <= win_steps)
    bias_prev = slope * (steps_prev * dil).astype(F32)
    bias_cur = slope * (steps_cur * dil).astype(F32)
    scale = DIL_HD ** -0.5
    for qb in range(nq):
        for r in range(dil):
            rows = pl.ds(qb * B * dil + r, B, stride=dil)
            if qb == 0:
                first = pl.ds(r, B, stride=dil)
                kp, vp, ok_prev = kp_ref[first, :], vp_ref[first, :], in_prev_first
            else:
                rows_prev = pl.ds((qb - 1) * B * dil + r, B, stride=dil)
                kp, vp, ok_prev = k_ref[rows_prev, :], v_ref[rows_prev, :], in_prev
            q = q_ref[rows, :].astype(BF16)
            sp = lax.dot_general(q, kp.astype(BF16), _NT, preferred_element_type=F32)
            sc = lax.dot_general(q, k_ref[rows, :].astype(BF16), _NT, preferred_element_type=F32)
            sp = jnp.where(ok_prev, sp * scale - bias_prev, -jnp.inf)
            sc = jnp.where(in_cur, sc * scale - bias_cur, -jnp.inf)
            m = jnp.maximum(jnp.max(sp, axis=-1, keepdims=True), jnp.max(sc, axis=-1, keepdims=True))
            ep = jnp.exp(sp - m)
            ec = jnp.exp(sc - m)
            l = jnp.sum(ep, axis=-1, keepdims=True) + jnp.sum(ec, axis=-1, keepdims=True)
            o = jnp.dot((ep / l).astype(BF16), vp.astype(BF16), preferred_element_type=F32)
            o = o + jnp.dot((ec / l).astype(BF16), v_ref[rows, :].astype(BF16), preferred_element_type=F32)
            o_ref[rows, :] = o
            lse_ref[rows, :] = jnp.broadcast_to(m + jnp.log(l), (B, DIL_HD))


def _dilated_prompt(proj, slopes, group, batch, seq):
    win, dil = DIL_GROUPS[group]
    span = DIL_BLOCK * dil
    assert seq % span == 0
    nq = _pick(seq // span, (4, 2, 1)) if dil == 1 else 1
    nj = seq // (span * nq)
    col = lambda off: off // DIL_HD + group * DIL_HPG

    def cur(off):
        return pl.BlockSpec((span * nq, DIL_HD), lambda b, j, hh: (b * nj + j, col(off) + hh))

    def prev(off):
        return pl.BlockSpec((span, DIL_HD), lambda b, j, hh: (jnp.maximum((b * nj + j) * nq - 1, 0), col(off) + hh))

    out_spec = pl.BlockSpec((span * nq, DIL_HD), lambda b, j, hh: (b * nj + j, hh))
    return pl.pallas_call(
        functools.partial(_dil_prompt_kernel, group=group, dil=dil, win_steps=win // dil, nq=nq),
        grid=(batch, nj, DIL_HPG),
        in_specs=[pl.BlockSpec(memory_space=pltpu.SMEM), cur(OFF_DQ), cur(OFF_DK), prev(OFF_DK), cur(OFF_DV), prev(OFF_DV)],
        out_specs=[out_spec, out_spec],
        out_shape=[jax.ShapeDtypeStruct((batch * seq, DIL_W), F32)] * 2,
        compiler_params=_params(32), name=f"dilated_prompt_g{group}")(slopes, proj, proj, proj, proj, proj)


def _dil_sample_kernel(q_ref, kn_ref, vn_ref, kc_ref, vc_ref, o_ref, lse_ref, ok_ref, ov_ref,
                       *, group, dil, win_steps, ts):
    H = DIL_HPG
    buf = kc_ref.shape[0] // H
    P = q_ref.shape[0]
    t = lax.broadcasted_iota(I32, (P, buf), 0)
    r = lax.broadcasted_iota(I32, (P, buf), 1)
    diff = buf + t - r
    span = win_steps * dil
    ok = ((diff & (dil - 1)) == 0) & (diff <= span) & (diff >= 0)
    bias_steps = diff.astype(F32)
    t1 = lax.broadcasted_iota(I32, (P, 1), 0)
    scale = DIL_HD ** -0.5
    for hh in range(DIL_HPG):
        slope = _alibi_slope(group * DIL_HPG + hh)
        c = slice(hh * DIL_HD, (hh + 1) * DIL_HD)
        q = q_ref[:, c]
        head_rows = pl.ds(hh, buf, stride=H)
        s = lax.dot_general(q.astype(BF16), kc_ref[head_rows, :].astype(BF16), _NT, preferred_element_type=F32)
        s = jnp.where(ok, s * scale - slope * bias_steps, -jnp.inf)
        m = jnp.max(s, axis=-1, keepdims=True)
        s_new = []
        for rp in range(ts):
            d = t1 - rp
            ok_n = (d >= 0) & ((d & (dil - 1)) == 0) & (d <= span)
            new_row = slice(rp * H + hh, rp * H + hh + 1)
            sn = jnp.sum(q * kn_ref[new_row, :], axis=-1, keepdims=True) * scale - slope * d.astype(F32)
            sn = jnp.where(ok_n, sn, -jnp.inf)
            s_new.append(sn)
            m = jnp.maximum(m, sn)
        e = jnp.exp(s - m)
        e_new = [jnp.exp(sn - m) for sn in s_new]
        l = jnp.sum(e, axis=-1, keepdims=True)
        for en in e_new:
            l = l + en
        o = jnp.dot((e / l).astype(BF16), vc_ref[head_rows, :].astype(BF16), preferred_element_type=F32)
        for rp in range(ts):
            o = o + (e_new[rp] / l) * vn_ref[rp * H + hh:rp * H + hh + 1, :]
        o_ref[:, c] = o
        lse_ref[:, c] = jnp.broadcast_to(m + jnp.log(l), (P, DIL_HD))
    ok_ref[0:(buf - ts) * H, :] = kc_ref[ts * H:buf * H, :]
    ok_ref[(buf - ts) * H:buf * H, :] = kn_ref[...]
    ov_ref[0:(buf - ts) * H, :] = vc_ref[ts * H:buf * H, :]
    ov_ref[(buf - ts) * H:buf * H, :] = vn_ref[...]


def _dilated_sample(q8, kn, vn, cache_k, cache_v, layer, group, ts):
    win, dil = DIL_GROUPS[group]
    depth, ns, buf = cache_k.shape[:3]
    assert buf == win and buf % dil == 0, "cached window must hold exactly one full window"
    flat = lambda c: c.reshape(depth, ns, buf * DIL_HPG, DIL_HD)
    small = pl.BlockSpec((None, SAMPLE_PAD, DIL_W), lambda n: (n, 0, 0))
    new = pl.BlockSpec((None, ts * DIL_HPG, DIL_HD), lambda n: (n, 0, 0))
    big_in = pl.BlockSpec((None, None, buf * DIL_HPG, DIL_HD), lambda n: (layer, n, 0, 0))
    big_out = pl.BlockSpec((None, buf * DIL_HPG, DIL_HD), lambda n: (n, 0, 0))
    o, lse, k_out, v_out = pl.pallas_call(
        functools.partial(_dil_sample_kernel, group=group, dil=dil, win_steps=win // dil, ts=ts),
        grid=(ns,), in_specs=[small, new, new, big_in, big_in],
        out_specs=[small, small, big_out, big_out],
        out_shape=[jax.ShapeDtypeStruct((ns, SAMPLE_PAD, DIL_W), F32)] * 2
        + [jax.ShapeDtypeStruct((ns, buf * DIL_HPG, DIL_HD), F32)] * 2,
        compiler_params=_params(56), name=f"dilated_sample_g{group}")(q8, kn, vn, flat(cache_k), flat(cache_v))
    return o, lse, k_out.reshape(ns, buf, DIL_HPG, DIL_HD), v_out.reshape(ns, buf, DIL_HPG, DIL_HD)


def _combine_kernel(o1_ref, o2_ref, o3_ref, l1_ref, l2_ref, l3_ref, out_ref):
    l1, l2, l3 = l1_ref[...], l2_ref[...], l3_ref[...]
    m = jnp.maximum(jnp.maximum(l1, l2), l3)
    e1, e2, e3 = jnp.exp(l1 - m), jnp.exp(l2 - m), jnp.exp(l3 - m)
    z = e1 + e2 + e3
    out = (e1 / z) * o1_ref[...] + (e2 / z) * o2_ref[...] + (e3 / z) * o3_ref[...]
    out_ref[...] = out.astype(out_ref.dtype)


def _combine_groups(outs, lses):
    m, w = outs[0].shape
    tm = _pick(m, (1024, 512, 256, 128, 64, 8))
    spec = pl.BlockSpec((tm, w), lambda i: (i, 0))
    return pl.pallas_call(_combine_kernel, grid=(m // tm,), in_specs=[spec] * 6, out_specs=spec,
                          out_shape=jax.ShapeDtypeStruct((m, w), BF16),
                          compiler_params=_params(40), name="combine_groups")(*outs, *lses)


def _mem_attn_head(q, k, v):
    s = lax.dot_general(q.astype(BF16), k.astype(BF16), _NT, preferred_element_type=F32) * (MEM_HD ** -0.5)
    e = jnp.exp(s - jnp.max(s, axis=-1, keepdims=True))
    p = e / jnp.sum(e, axis=-1, keepdims=True)
    return jnp.dot(p.astype(BF16), v.astype(BF16), preferred_element_type=F32)


def _mem_attn_kernel(q_ref, k_ref, v_ref, o_ref):
    for h in range(MEM_HEADS):
        c = slice(h * MEM_HD, (h + 1) * MEM_HD)
        o_ref[:, c] = _mem_attn_head(q_ref[:, c], k_ref[:, c], v_ref[:, c]).astype(o_ref.dtype)


def _mem_attn_cached_kernel(q_ref, k_ref, v_ref, o_ref):
    chunks = MEM_HD // 128
    mem_len = k_ref.shape[0] // (MEM_HEADS * chunks)
    for h in range(MEM_HEADS):
        rows = [pl.ds(h * chunks + c, mem_len, stride=MEM_HEADS * chunks) for c in range(chunks)]
        cols = [slice(h * MEM_HD + c * 128, h * MEM_HD + (c + 1) * 128) for c in range(chunks)]
        s = sum(lax.dot_general(q_ref[:, cols[c]].astype(BF16), k_ref[rows[c], :].astype(BF16), _NT,
                                preferred_element_type=F32) for c in range(chunks)) * (MEM_HD ** -0.5)
        e = jnp.exp(s - jnp.max(s, axis=-1, keepdims=True))
        p = (e / jnp.sum(e, axis=-1, keepdims=True)).astype(BF16)
        for c in range(chunks):
            o_ref[:, cols[c]] = jnp.dot(p, v_ref[rows[c], :].astype(BF16),
                                        preferred_element_type=F32).astype(o_ref.dtype)


def _memory_attend_prompt(proj, mkv, batch, seq):
    mem_len = mkv.shape[1]
    tq = _pick(seq, (512, 256, 128))
    nq = seq // tq
    return pl.pallas_call(
        _mem_attn_kernel, grid=(batch, nq),
        in_specs=[pl.BlockSpec((tq, MEM_W), lambda b, i: (b * nq + i, OFF_MQ // MEM_W)),
                  pl.BlockSpec((None, mem_len, MEM_W), lambda b, i: (b, 0, 0)),
                  pl.BlockSpec((None, mem_len, MEM_W), lambda b, i: (b, 0, 1))],
        out_specs=pl.BlockSpec((tq, MEM_W), lambda b, i: (b * nq + i, 0)),
        out_shape=jax.ShapeDtypeStruct((batch * seq, MEM_W), BF16),
        compiler_params=_params(40), name="memory_attend_prompt")(proj, mkv, mkv)


def _memory_attend_sample(q8, mem_k, mem_v, layer):
    depth, ns, mem_len = mem_k.shape[:3]
    rows = mem_len * MEM_W // 128
    flat = lambda c: c.reshape(depth, ns, rows, 128)
    small = pl.BlockSpec((None, SAMPLE_PAD, MEM_W), lambda n: (n, 0, 0))
    big = pl.BlockSpec((None, None, rows, 128), lambda n: (layer, n, 0, 0))
    return pl.pallas_call(
        _mem_attn_cached_kernel, grid=(ns,), in_specs=[small, big, big], out_specs=small,
        out_shape=jax.ShapeDtypeStruct((ns, SAMPLE_PAD, MEM_W), F32),
        compiler_params=_params(24), name="memory_attend_sample")(q8, flat(mem_k), flat(mem_v))


def _mix_kernel(ar_ref, ad_ref, am_ref, wr_ref, wd_ref, wm_ref, gr_ref, gd_ref, gm_ref, o_ref):
    r = jnp.dot(ar_ref[...], wr_ref[...], preferred_element_type=F32)
    d = jnp.dot(ad_ref[...], wd_ref[...], preferred_element_type=F32)
    m = jnp.dot(am_ref[...], wm_ref[...], preferred_element_type=F32)
    mix = jax.nn.sigmoid(gr_ref[...]) * r + jax.nn.sigmoid(gd_ref[...]) * d + jax.nn.sigmoid(gm_ref[...]) * m
    o_ref[...] = mix.astype(o_ref.dtype)


def _branch_mix(a_r, a_d, a_m, w_r, w_d, w_m, proj):
    t = a_r.shape[0]
    tm = _pick(t, (832, 640, 512, 416, 320, 256, 128, 64, 8))
    tn = 512
    nj = D_MODEL // tn
    lhs = lambda w: pl.BlockSpec((tm, w), lambda i, j: (i, 0))
    rhs = lambda w: pl.BlockSpec((w, tn), lambda i, j: (0, j))
    gate = lambda b: pl.BlockSpec((tm, tn), lambda i, j: (i, OFF_GATE // tn + b * nj + j))
    return pl.pallas_call(
        _mix_kernel, grid=(t // tm, nj),
        in_specs=[lhs(RET_V), lhs(DIL_W), lhs(MEM_W), rhs(RET_V), rhs(DIL_W), rhs(MEM_W), gate(0), gate(1), gate(2)],
        out_specs=pl.BlockSpec((tm, tn), lambda i, j: (i, j)),
        out_shape=jax.ShapeDtypeStruct((t, D_MODEL), BF16),
        compiler_params=_params(56), name="branch_mix")(a_r, a_d, a_m, w_r, w_d, w_m, proj, proj, proj)


def _peer_topk_kernel(q_ref, keys_ref, i_ref, j_ref, g_ref):
    tk = q_ref.shape[0]
    K = PEER_TOPK
    iota_k = lax.broadcasted_iota(I32, (K, tk), 0)

    def top_k_rows(scores, nrows):
        iota = lax.broadcasted_iota(I32, (nrows, tk), 0)

        def body(k, carry):
            sel = iota_k == k
            out = []
            for s, top_s, top_i in carry:
                m = jnp.max(s, axis=0, keepdims=True)
                pos = jnp.min(jnp.where(s == m, iota, nrows), axis=0, keepdims=True)
                out.append((jnp.where(iota == pos, -jnp.inf, s), jnp.where(sel, m, top_s), jnp.where(sel, pos, top_i)))
            return tuple(out)

        init = tuple((s, jnp.zeros((K, tk), F32), jnp.zeros((K, tk), I32)) for s in scores)
        return [(top_s, top_i) for _, top_s, top_i in lax.fori_loop(0, K, body, init)]

    halves = []
    for c in range(2):
        qc = q_ref[:, c * PEER_NKEYS:(c + 1) * PEER_NKEYS].astype(BF16)
        halves.append(lax.dot_general(keys_ref[0, c].astype(BF16), qc, _NT, preferred_element_type=F32))
    (s1, i1), (s2, i2) = top_k_rows(halves, PEER_NKEYS)
    cand = jnp.concatenate([s1[a:a + 1, :] + s2 for a in range(K)], axis=0)
    (best_s, best_p), = top_k_rows([cand], K * K)
    pa = jnp.right_shift(best_p, K.bit_length() - 1)
    pb = best_p - pa * K
    ei = jnp.zeros((K, tk), I32)
    ej = jnp.zeros((K, tk), I32)
    for a in range(K):
        ei = jnp.where(pa == a, i1[a:a + 1, :], ei)
        ej = jnp.where(pb == a, i2[a:a + 1, :], ej)
    e = jnp.exp(best_s - jnp.max(best_s, axis=0, keepdims=True))
    i_ref[...] = ei.astype(F32)
    j_ref[...] = ej.astype(F32)
    g_ref[...] = e / jnp.sum(e, axis=0, keepdims=True)


def _peer_route(q, sub_keys):
    t = q.shape[0]
    tk = _pick(t, (640, 512, 256, 128))
    out_spec = pl.BlockSpec((PEER_TOPK, tk), lambda i, h: (h, i))
    return pl.pallas_call(
        _peer_topk_kernel, grid=(t // tk, PEER_HEADS),
        in_specs=[pl.BlockSpec((tk, PEER_DQ), lambda i, h: (i, h)),
                  pl.BlockSpec((1, 2, PEER_NKEYS, PEER_DQ // 2), lambda i, h: (h, 0, 0, 0))],
        out_specs=[out_spec] * 3,
        out_shape=[jax.ShapeDtypeStruct((PEER_HEADS * PEER_TOPK, t), F32)] * 3,
        compiler_params=_params(24), name="peer_route")(q, sub_keys)


def _peer_wbuild_kernel(i_ref, j_ref, g_ref, w_ref, it_scr, jt_scr, gt_scr, wt_scr):
    tw = w_ref.shape[0]
    it_scr[...] = i_ref[...].T
    jt_scr[...] = j_ref[...].T
    gt_scr[...] = g_ref[...].T
    n = PEER_NKEYS
    iota_r = lax.broadcasted_iota(I32, (n, i_ref.shape[0]), 0).astype(F32)

    def body(t, carry):
        irow = it_scr[pl.ds(t, 1), :]
        jrow = jt_scr[pl.ds(t, 1), :]
        grow = gt_scr[pl.ds(t, 1), :]
        a_t = jnp.where(iota_r == irow, grow, 0.0).astype(BF16)
        b_t = jnp.where(iota_r == jrow, 1.0, 0.0).astype(BF16)
        w = lax.dot_general(a_t, b_t, _NT, preferred_element_type=F32)
        wt_scr[pl.ds(pl.multiple_of(t * W_ROW_PITCH, 8), n), :] = w
        return carry

    lax.fori_loop(0, tw, body, 0, unroll=4)
    for i in range(n):
        w_ref[:, i * n:(i + 1) * n] = wt_scr[pl.ds(i, tw, stride=W_ROW_PITCH), :].astype(BF16)


def _peer_wbuild(ei, ej, g):
    nsel, t = ei.shape
    tw = 128
    in_spec = pl.BlockSpec((nsel, tw), lambda i: (0, i))
    n = PEER_NKEYS
    return pl.pallas_call(
        _peer_wbuild_kernel, grid=(t // tw,), in_specs=[in_spec] * 3,
        out_specs=pl.BlockSpec((tw, n * n), lambda i: (i, 0)),
        out_shape=jax.ShapeDtypeStruct((t, n * n), BF16),
        scratch_shapes=[pltpu.VMEM((tw, nsel), F32)] * 3 + [pltpu.VMEM((tw * W_ROW_PITCH, n), F32)],
        compiler_params=_params(32), name="peer_wbuild")(ei, ej, g)


def _gelu_tanh(x):
    return x * (0.5 * (1.0 + jnp.tanh(math.sqrt(2.0 / math.pi) * (x + 0.044715 * (x * x * x)))))


def _peer_mlp_kernel(x_ref, u_ref, v_ref, w_ref, o_ref):
    e = pl.program_id(1)
    act = lax.dot_general(x_ref[...], u_ref[...], _NT, preferred_element_type=F32)
    p = (w_ref[...].astype(F32) * _gelu_tanh(act)).astype(BF16)
    part = jnp.dot(p, v_ref[...], preferred_element_type=F32)

    @pl.when(e == 0)
    def _():
        o_ref[...] = part

    @pl.when(e > 0)
    def _():
        o_ref[...] += part


def _peer_mlp(n2, u, v, w):
    t, d = n2.shape
    ne = u.shape[0]
    tm = _pick(t, (416, 320, 256, 128, 64, 8))
    eb = 512
    return pl.pallas_call(
        _peer_mlp_kernel, grid=(t // tm, ne // eb),
        in_specs=[pl.BlockSpec((tm, d), lambda i, e: (i, 0)),
                  pl.BlockSpec((eb, d), lambda i, e: (e, 0)),
                  pl.BlockSpec((eb, d), lambda i, e: (e, 0)),
                  pl.BlockSpec((tm, eb), lambda i, e: (i, e))],
        out_specs=pl.BlockSpec((tm, d), lambda i, e: (i, 0)),
        out_shape=jax.ShapeDtypeStruct((t, d), F32),
        compiler_params=_params(60), name="peer_mlp")(n2, u, v, w)


def _pad_sample_rows(a, ns, ts):
    a = a.reshape(ns, ts, a.shape[-1])
    return jnp.pad(a, ((0, 0), (0, SAMPLE_PAD - ts), (0, 0)))


def kernel(x_prompt, x_sample, mem_prompt, cache_ret_state, cache_win_k1, cache_win_v1, cache_win_k2, cache_win_v2, cache_win_k3, cache_win_v3, cache_mem_k, cache_mem_v, norm1_w, w_in, ret_gn_w, w_ret_o, w_dil_o, mem_norm_w, w_mem_kv, w_mem_o, w_out, norm2_w, w_peer_q, peer_sub_keys, peer_u, peer_v, final_norm_w):
    batch, seq, d = x_prompt.shape
    ns, ts, _ = x_sample.shape
    mem_len = mem_prompt.shape[1]
    depth = norm1_w.shape[0]
    assert d == D_MODEL and seq % RET_CHUNK == 0 and ts <= SAMPLE_PAD
    tp, tsamp = batch * seq, ns * ts
    win_k = (cache_win_k1, cache_win_k2, cache_win_k3)
    win_v = (cache_win_v1, cache_win_v2, cache_win_v3)

    slopes = jnp.asarray([_alibi_slope(h) for h in range(DIL_HEADS)], F32)
    x_all = jnp.concatenate([x_prompt.reshape(tp, d), x_sample.reshape(tsamp, d)], axis=0)
    prompt_states, sample_states = [], []
    for l in range(depth):
        proj = _matmul(_rmsnorm(x_all, norm1_w[l], BF16), w_in[l].astype(BF16),
                       tm_cands=(1664, 832, 640, 512, 416, 320, 256, 128, 64, 8), name="input_projection")
        proj_s = proj[tp:]

        ar_p, ret_p = _retention_prompt(proj, ret_gn_w[l], batch, seq)
        ar_s, ret_s = _retention_sample(proj_s, ret_gn_w[l], cache_ret_state[l], ts)

        outs_p, lses_p, outs_s, lses_s, bufs_p, bufs_s = [], [], [], [], [], []
        for g, (win, dil) in enumerate(DIL_GROUPS):
            o, lse = _dilated_prompt(proj, slopes, g, batch, seq)
            outs_p.append(o)
            lses_p.append(lse)
            keep = min(win, seq)
            for off in (OFF_DK, OFF_DV):
                c0 = off + g * DIL_W
                tail = [proj[b * seq + seq - keep:(b + 1) * seq, c0:c0 + DIL_W] for b in range(batch)]
                bufs_p.append(jnp.stack(tail, axis=0).reshape(batch, keep, DIL_HPG, DIL_HD))
            q_s, k_s, v_s = (proj_s[:, off + g * DIL_W: off + (g + 1) * DIL_W] for off in (OFF_DQ, OFF_DK, OFF_DV))
            o8, lse8, k_new, v_new = _dilated_sample(
                _pad_sample_rows(q_s, ns, ts), k_s.reshape(ns, ts * DIL_HPG, DIL_HD), v_s.reshape(ns, ts * DIL_HPG, DIL_HD),
                win_k[g], win_v[g], l, g, ts)
            outs_s.append(o8[:, :ts].reshape(tsamp, DIL_W))
            lses_s.append(lse8[:, :ts].reshape(tsamp, DIL_W))
            bufs_s += [k_new, v_new]
        ad_p = _combine_groups(outs_p, lses_p)
        ad_s = _combine_groups(outs_s, lses_s)

        mem_n = _rmsnorm(mem_prompt.reshape(batch * mem_len, d), mem_norm_w[l], BF16)
        mkv = _matmul(mem_n, w_mem_kv[l].astype(BF16), name="memory_kv").reshape(batch, mem_len, 2 * MEM_W)
        am_p = _memory_attend_prompt(proj, mkv, batch, seq)
        mq8 = _pad_sample_rows(proj_s[:, OFF_MQ:OFF_MQ + MEM_W], ns, ts)
        am_s = _memory_attend_sample(mq8, cache_mem_k, cache_mem_v, l)
        am_s = am_s[:, :ts].reshape(tsamp, MEM_W).astype(BF16)

        a_r = jnp.concatenate([ar_p, ar_s], axis=0)
        a_d = jnp.concatenate([ad_p, ad_s], axis=0)
        a_m = jnp.concatenate([am_p, am_s], axis=0)
        mix = _branch_mix(a_r, a_d, a_m, w_ret_o[l].astype(BF16), w_dil_o[l].astype(BF16),
                          w_mem_o[l].astype(BF16), proj)
        h = _matmul(mix, w_out[l].astype(BF16), res=x_all, name="output_projection")

        n2 = _rmsnorm(h, norm2_w[l], BF16)
        pq = _matmul(n2, w_peer_q[l].astype(BF16), name="peer_query")
        ei, ej, gw = _peer_route(pq, peer_sub_keys[l])
        w_dense = _peer_wbuild(ei, ej, gw)
        peer = _peer_mlp(n2, peer_u[l].astype(BF16), peer_v[l].astype(BF16), w_dense)

        mk = mkv[:, :, :MEM_W].reshape(batch, mem_len, MEM_HEADS, MEM_HD)
        mv = mkv[:, :, MEM_W:].reshape(batch, mem_len, MEM_HEADS, MEM_HD)
        prompt_states.append((ret_p, *bufs_p, mk, mv))
        sample_states.append((ret_s, *bufs_s))
        if l + 1 < depth:
            x_all = _add(h, peer)

    y_prompt = _add_rmsnorm(h, peer, final_norm_w, 0, tp).reshape(batch, seq, d)
    y_sample = _add_rmsnorm(h, peer, final_norm_w, tp, tsamp).reshape(ns, ts, d)
    p_out = [jnp.stack(s, axis=0) for s in zip(*prompt_states)]
    s_out = [jnp.stack(s, axis=0) for s in zip(*sample_states)]
    return (y_prompt, y_sample, *p_out, *s_out)
```

```python
import functools
import math

import jax
import jax.numpy as jnp
from jax import lax
from jax.experimental import pallas as pl
from jax.experimental.pallas import tpu as pltpu

F32, BF16, I32 = jnp.float32, jnp.bfloat16, jnp.int32

D_MODEL = 4096
RET_HEADS, RET_DK, RET_DV, RET_CHUNK = 8, 128, 256, 128
DIL_GROUPS = ((128, 1), (512, 4), (2048, 16))
DIL_HPG, DIL_HD, DIL_BLOCK = 4, 128, 128
DIL_HEADS = DIL_HPG * len(DIL_GROUPS)
ALIBI_MAX_EXP = 8.0
MEM_HEADS, MEM_HD = 4, 384
PEER_HEADS, PEER_NKEYS, PEER_TOPK, PEER_DQ = 8, 128, 16, 256
NORM_EPS = 1e-6

RET_QK = RET_HEADS * RET_DK
RET_V = RET_HEADS * RET_DV
DIL_W = DIL_HPG * DIL_HD
DIL_ALL = DIL_HEADS * DIL_HD
MEM_W = MEM_HEADS * MEM_HD
OFF_RQ = 0
OFF_RK = OFF_RQ + RET_QK
OFF_RV = OFF_RK + RET_QK
OFF_RG = OFF_RV + RET_V
OFF_DQ = OFF_RG + RET_V
OFF_DK = OFF_DQ + DIL_ALL
OFF_DV = OFF_DK + DIL_ALL
OFF_MQ = OFF_DV + DIL_ALL
OFF_GATE = OFF_MQ + MEM_W
PROJ_W = OFF_GATE + 3 * D_MODEL

SAMPLE_PAD = 8
W_ROW_PITCH = PEER_NKEYS + 8
V7X_VMEM_BYTES = 64 * 1024 * 1024
MIB = 1024 * 1024

_NT = (((1,), (1,)), ((), ()))


def _pick(n, cands):
    for c in cands:
        if n % c == 0:
            return c
    raise ValueError(f"no tile in {cands} divides {n}")


def _params(vmem_mib):
    return pltpu.CompilerParams(vmem_limit_bytes=min(vmem_mib * MIB, V7X_VMEM_BYTES - 2 * MIB))


def _rmsnorm_kernel(x_ref, w_ref, o_ref):
    x = x_ref[...]
    ms = jnp.mean(x * x, axis=-1, keepdims=True)
    o_ref[...] = (x * lax.rsqrt(ms + NORM_EPS) * w_ref[...]).astype(o_ref.dtype)


def _rmsnorm(x, w, out_dtype):
    m, d = x.shape
    tm = _pick(m, (320, 256, 128, 64, 8))
    return pl.pallas_call(
        _rmsnorm_kernel, grid=(m // tm,),
        in_specs=[pl.BlockSpec((tm, d), lambda i: (i, 0)), pl.BlockSpec((1, d), lambda i: (0, 0))],
        out_specs=pl.BlockSpec((tm, d), lambda i: (i, 0)),
        out_shape=jax.ShapeDtypeStruct((m, d), out_dtype),
        compiler_params=_params(48), name="rmsnorm")(x, w.reshape(1, d))


def _add_rmsnorm_kernel(a_ref, b_ref, w_ref, o_ref):
    x = a_ref[...] + b_ref[...]
    ms = jnp.mean(x * x, axis=-1, keepdims=True)
    o_ref[...] = x * lax.rsqrt(ms + NORM_EPS) * w_ref[...]


def _add_rmsnorm(a, b, w, row0, rows):
    d = a.shape[1]
    tm = _pick(math.gcd(row0, rows) if row0 else rows, (256, 128, 64, 8))
    blk0 = row0 // tm
    spec = pl.BlockSpec((tm, d), lambda i: (blk0 + i, 0))
    return pl.pallas_call(
        _add_rmsnorm_kernel, grid=(rows // tm,),
        in_specs=[spec, spec, pl.BlockSpec((1, d), lambda i: (0, 0))],
        out_specs=pl.BlockSpec((tm, d), lambda i: (i, 0)),
        out_shape=jax.ShapeDtypeStruct((rows, d), F32),
        compiler_params=_params(56), name="add_rmsnorm")(a, b, w.reshape(1, d))


def _add_kernel(a_ref, b_ref, o_ref):
    o_ref[...] = a_ref[...] + b_ref[...]


def _add(a, b):
    m, d = a.shape
    tm = _pick(m, (320, 256, 128, 64, 8))
    spec = pl.BlockSpec((tm, d), lambda i: (i, 0))
    return pl.pallas_call(_add_kernel, grid=(m // tm,), in_specs=[spec, spec], out_specs=spec,
                          out_shape=jax.ShapeDtypeStruct((m, d), F32),
                          compiler_params=_params(56), name="residual_add")(a, b)


def _mm_kernel(a_ref, b_ref, o_ref):
    o_ref[...] = jnp.dot(a_ref[...], b_ref[...].astype(a_ref.dtype), preferred_element_type=F32).astype(o_ref.dtype)


def _mm_res_kernel(a_ref, b_ref, r_ref, o_ref):
    o_ref[...] = r_ref[...] + jnp.dot(a_ref[...], b_ref[...], preferred_element_type=F32)


def _matmul(a, b, res=None, tm_cands=(832, 640, 512, 416, 320, 256, 128, 64, 8), tn=512, name="matmul"):
    m, k = a.shape
    n = b.shape[1]
    tm = _pick(m, tm_cands)
    tn = _pick(n, (tn, 256, 128))
    in_specs = [pl.BlockSpec((tm, k), lambda i, j: (i, 0)), pl.BlockSpec((k, tn), lambda i, j: (0, j))]
    args = [a, b]
    body = _mm_kernel
    if res is not None:
        in_specs.append(pl.BlockSpec((tm, tn), lambda i, j: (i, j)))
        args.append(res)
        body = _mm_res_kernel
    return pl.pallas_call(
        body, grid=(m // tm, n // tn), in_specs=in_specs,
        out_specs=pl.BlockSpec((tm, tn), lambda i, j: (i, j)),
        out_shape=jax.ShapeDtypeStruct((m, n), F32),
        compiler_params=_params(56), name=name)(*args)


def _ret_log_decay(h):
    return math.log(1.0 - 2.0 ** (-5.0 - h))


def _group_norm_gate(o, gnw, rg):
    mu = jnp.mean(o, axis=-1, keepdims=True)
    d = o - mu
    var = jnp.mean(d * d, axis=-1, keepdims=True)
    y = d * lax.rsqrt(var + NORM_EPS) * gnw
    return (y * (rg * jax.nn.sigmoid(rg))).astype(BF16)


def _ret_prompt_kernel(q_ref, k_ref, v_ref, rg_ref, gnw_ref, ar_ref, st_ref, state_scr):
    c = pl.program_id(1)
    L = RET_CHUNK

    @pl.when(c == 0)
    def _():
        state_scr[...] = jnp.zeros_like(state_scr)

    diff = (lax.broadcasted_iota(I32, (L, L), 0) - lax.broadcasted_iota(I32, (L, L), 1)).astype(F32)
    row_k = lax.broadcasted_iota(I32, (L, RET_DK), 0).astype(F32)
    row_v = lax.broadcasted_iota(I32, (L, RET_DV), 0).astype(F32)
    def first_matmuls(h):
        lg = _ret_log_decay(h)
        ck = slice(h * RET_DK, (h + 1) * RET_DK)
        q = q_ref[:, ck].astype(BF16)
        k = k_ref[:, ck] * (RET_DK ** -0.5)
        v = v_ref[:, h * RET_DV:(h + 1) * RET_DV].astype(BF16)
        s = lax.dot_general(q, k.astype(BF16), _NT, preferred_element_type=F32)
        st = state_scr[h]
        o_cross = jnp.dot(q, st.astype(BF16), preferred_element_type=F32)
        kd = (k * jnp.exp(lg * (L - 1.0 - row_k))).T.astype(BF16)
        state_scr[h] = st * math.exp(lg * L) + jnp.dot(kd, v, preferred_element_type=F32)
        return s, o_cross, v

    def decay_scores(h, s, o_cross, v):
        lg = _ret_log_decay(h)
        decay = jnp.where(diff >= 0, jnp.exp(lg * jnp.maximum(diff, 0.0)), 0.0)
        return (s * decay).astype(BF16), o_cross * jnp.exp(lg * (row_v + 1.0)), v

    def finish(h, inner, o_cross, v):
        cv = slice(h * RET_DV, (h + 1) * RET_DV)
        o = jnp.dot(inner, v, preferred_element_type=F32) + o_cross
        ar_ref[:, cv] = _group_norm_gate(o, gnw_ref[:, cv], rg_ref[:, cv])

    stage1, stage2 = {}, {}
    for step in range(RET_HEADS + 2):
        if step < RET_HEADS:
            stage1[step] = first_matmuls(step)
        if 0 <= step - 1 < RET_HEADS:
            stage2[step - 1] = decay_scores(step - 1, *stage1.pop(step - 1))
        if step - 2 >= 0:
            finish(step - 2, *stage2.pop(step - 2))

    @pl.when(c == pl.num_programs(1) - 1)
    def _():
        st_ref[0] = state_scr[...]


def _retention_prompt(proj, gnw, batch, seq):
    nc = seq // RET_CHUNK
    row = lambda n, c: n * nc + c
    return pl.pallas_call(
        _ret_prompt_kernel, grid=(batch, nc),
        in_specs=[pl.BlockSpec((RET_CHUNK, RET_QK), lambda n, c: (row(n, c), OFF_RQ // RET_QK)),
                  pl.BlockSpec((RET_CHUNK, RET_QK), lambda n, c: (row(n, c), OFF_RK // RET_QK)),
                  pl.BlockSpec((RET_CHUNK, RET_V), lambda n, c: (row(n, c), OFF_RV // RET_V)),
                  pl.BlockSpec((RET_CHUNK, RET_V), lambda n, c: (row(n, c), OFF_RG // RET_V)),
                  pl.BlockSpec((1, RET_V), lambda n, c: (0, 0))],
        out_specs=[pl.BlockSpec((RET_CHUNK, RET_V), lambda n, c: (row(n, c), 0)),
                   pl.BlockSpec((1, RET_HEADS, RET_DK, RET_DV), lambda n, c: (n, 0, 0, 0))],
        out_shape=[jax.ShapeDtypeStruct((batch * seq, RET_V), BF16),
                   jax.ShapeDtypeStruct((batch, RET_HEADS, RET_DK, RET_DV), F32)],
        scratch_shapes=[pltpu.VMEM((RET_HEADS, RET_DK, RET_DV), F32)],
        compiler_params=_params(32), name="retention_prompt")(proj, proj, proj, proj, gnw.reshape(1, RET_V))


def _ret_sample_kernel(q_ref, k_ref, v_ref, rg_ref, gnw_ref, st_in_ref, ar_ref, st_out_ref, o_scr, *, ts, ns_blk):
    i = pl.program_id(0)
    R = q_ref.shape[0]

    def sample_and_pos(shape, axis):
        r = lax.broadcasted_iota(I32, shape, axis).astype(F32)
        sid = jnp.floor((r + 0.5) * (1.0 / ts))
        return sid.astype(I32), r - sid * ts

    sid_r, t_r = sample_and_pos((R, R), 0)
    sid_c, t_c = sample_and_pos((R, R), 1)
    sid_k, t_k = sample_and_pos((R, RET_DK), 0)
    _, t_v = sample_and_pos((R, RET_DV), 0)
    for h in range(RET_HEADS):
        lg = _ret_log_decay(h)
        ck = slice(h * RET_DK, (h + 1) * RET_DK)
        cv = slice(h * RET_DV, (h + 1) * RET_DV)
        q = q_ref[:, ck]
        k = k_ref[:, ck] * (RET_DK ** -0.5)
        v = v_ref[:, cv].astype(BF16)

        @pl.when(i == 0)
        def _():
            s = lax.dot_general(q.astype(BF16), k.astype(BF16), _NT, preferred_element_type=F32)
            dt = t_r - t_c
            decay = jnp.where((sid_r == sid_c) & (dt >= 0), jnp.exp(lg * jnp.maximum(dt, 0.0)), 0.0)
            o_scr[:, cv] = jnp.dot((s * decay).astype(BF16), v, preferred_element_type=F32)

        q_dec = jnp.exp(lg * (t_v + 1.0))
        kd = k * jnp.exp(lg * (ts - 1.0 - t_k))
        for j in range(ns_blk):
            mine = sid_k == i * ns_blk + j
            st = st_in_ref[j, h]
            qm = jnp.where(mine, q, 0.0).astype(BF16)
            o_scr[:, cv] += jnp.dot(qm, st.astype(BF16), preferred_element_type=F32) * q_dec
            kdm = jnp.where(mine, kd, 0.0).T.astype(BF16)
            st_out_ref[j, h] = st * math.exp(lg * ts) + jnp.dot(kdm, v, preferred_element_type=F32)

    @pl.when(i == pl.num_programs(0) - 1)
    def _():
        for h in range(RET_HEADS):
            cv = slice(h * RET_DV, (h + 1) * RET_DV)
            ar_ref[:, cv] = _group_norm_gate(o_scr[:, cv], gnw_ref[:, cv], rg_ref[:, cv])


def _retention_sample(proj_s, gnw, state, ts):
    rows = proj_s.shape[0]
    ns = state.shape[0]
    ns_blk = _pick(ns, (4, 2, 1))
    return pl.pallas_call(
        functools.partial(_ret_sample_kernel, ts=ts, ns_blk=ns_blk), grid=(ns // ns_blk,),
        in_specs=[pl.BlockSpec((rows, RET_QK), lambda i: (0, OFF_RQ // RET_QK)),
                  pl.BlockSpec((rows, RET_QK), lambda i: (0, OFF_RK // RET_QK)),
                  pl.BlockSpec((rows, RET_V), lambda i: (0, OFF_RV // RET_V)),
                  pl.BlockSpec((rows, RET_V), lambda i: (0, OFF_RG // RET_V)),
                  pl.BlockSpec((1, RET_V), lambda i: (0, 0)),
                  pl.BlockSpec((ns_blk, RET_HEADS, RET_DK, RET_DV), lambda i: (i, 0, 0, 0))],
        out_specs=[pl.BlockSpec((rows, RET_V), lambda i: (0, 0)),
                   pl.BlockSpec((ns_blk, RET_HEADS, RET_DK, RET_DV), lambda i: (i, 0, 0, 0))],
        out_shape=[jax.ShapeDtypeStruct((rows, RET_V), BF16),
                   jax.ShapeDtypeStruct(state.shape, F32)],
        scratch_shapes=[pltpu.VMEM((rows, RET_V), F32)],
        compiler_params=_params(40), name="retention_sample")(
            proj_s, proj_s, proj_s, proj_s, gnw.reshape(1, RET_V), state)


def _alibi_slope(head):
    return 2.0 ** (-ALIBI_MAX_EXP * (head + 1.0) / DIL_HEADS)


def _dil_prompt_kernel(slope_ref, q_ref, k_ref, kp_ref, v_ref, vp_ref, o_ref, lse_ref, *, group, dil, win_steps, nq):
    j = pl.program_id(1)
    hh = pl.program_id(2)
    slope = slope_ref[group * DIL_HPG + hh]
    B = DIL_BLOCK
    qi = lax.broadcasted_iota(I32, (B, B), 0)
    kj = lax.broadcasted_iota(I32, (B, B), 1)
    steps_prev = qi + B - kj
    steps_cur = qi - kj
    in_prev = steps_prev <= win_steps
    in_prev_first = in_prev & (jnp.full((B, B), j, I32) > 0)
    in_cur = (steps_cur >= 0) & (steps_cur <= win_steps)
    bias_prev = slope * (steps_prev * dil).astype(F32)
    bias_cur = slope * (steps_cur * dil).astype(F32)
    scale = DIL_HD ** -0.5
    tiles = [(qb, r) for qb in range(nq) for r in range(dil)]

    def rows_of(qb, r):
        return pl.ds(qb * B * dil + r, B, stride=dil)

    def prev_of(ref, first_ref, qb, r):
        return first_ref[rows_of(0, r), :] if qb == 0 else ref[rows_of(qb - 1, r), :]

    def scores(qb, r):
        q = q_ref[rows_of(qb, r), :].astype(BF16)
        sp = lax.dot_general(q, prev_of(k_ref, kp_ref, qb, r).astype(BF16), _NT, preferred_element_type=F32)
        sc = lax.dot_general(q, k_ref[rows_of(qb, r), :].astype(BF16), _NT, preferred_element_type=F32)
        return sp, sc

    def softmax(qb, sp, sc):
        sp = jnp.where(in_prev_first if qb == 0 else in_prev, sp * scale - bias_prev, -jnp.inf)
        sc = jnp.where(in_cur, sc * scale - bias_cur, -jnp.inf)
        m = jnp.maximum(jnp.max(sp, axis=-1, keepdims=True), jnp.max(sc, axis=-1, keepdims=True))
        ep = jnp.exp(sp - m)
        ec = jnp.exp(sc - m)
        l = jnp.sum(ep, axis=-1, keepdims=True) + jnp.sum(ec, axis=-1, keepdims=True)
        return (ep / l).astype(BF16), (ec / l).astype(BF16), m + jnp.log(l)

    def finish(qb, r, pp, pc, lse):
        o = jnp.dot(pp, prev_of(v_ref, vp_ref, qb, r).astype(BF16), preferred_element_type=F32)
        o = o + jnp.dot(pc, v_ref[rows_of(qb, r), :].astype(BF16), preferred_element_type=F32)
        o_ref[rows_of(qb, r), :] = o
        lse_ref[rows_of(qb, r), :] = jnp.broadcast_to(lse, (B, DIL_HD))

    s_ready, p_ready = {}, {}
    lag_softmax, lag_finish = 2, 4
    for step in range(len(tiles) + lag_finish):
        if step < len(tiles):
            s_ready[step] = scores(*tiles[step])
        if 0 <= step - lag_softmax < len(tiles):
            p_ready[step - lag_softmax] = softmax(tiles[step - lag_softmax][0], *s_ready.pop(step - lag_softmax))
        if 0 <= step - lag_finish < len(tiles):
            finish(*tiles[step - lag_finish], *p_ready.pop(step - lag_finish))


def _dilated_prompt(proj, slopes, group, batch, seq):
    win, dil = DIL_GROUPS[group]
    span = DIL_BLOCK * dil
    assert seq % span == 0
    nq = _pick(seq // span, [n for n in (8, 4, 2, 1) if n * dil <= 16])
    nj = seq // (span * nq)
    col = lambda off: off // DIL_HD + group * DIL_HPG

    def cur(off):
        return pl.BlockSpec((span * nq, DIL_HD), lambda b, j, hh: (b * nj + j, col(off) + hh))

    def prev(off):
        return pl.BlockSpec((span, DIL_HD), lambda b, j, hh: (jnp.maximum((b * nj + j) * nq - 1, 0), col(off) + hh))

    out_spec = pl.BlockSpec((span * nq, DIL_HD), lambda b, j, hh: (b * nj + j, hh))
    return pl.pallas_call(
        functools.partial(_dil_prompt_kernel, group=group, dil=dil, win_steps=win // dil, nq=nq),
        grid=(batch, nj, DIL_HPG),
        in_specs=[pl.BlockSpec(memory_space=pltpu.SMEM), cur(OFF_DQ), cur(OFF_DK), prev(OFF_DK), cur(OFF_DV), prev(OFF_DV)],
        out_specs=[out_spec, out_spec],
        out_shape=[jax.ShapeDtypeStruct((batch * seq, DIL_W), F32)] * 2,
        compiler_params=_params(32), name=f"dilated_prompt_g{group}")(slopes, proj, proj, proj, proj, proj)


def _dil_sample_kernel(q_ref, kn_ref, vn_ref, kc_ref, vc_ref, o_ref, lse_ref, ok_ref, ov_ref,
                       *, group, dil, win_steps, ts):
    H = DIL_HPG
    buf = kc_ref.shape[0] // H
    P = q_ref.shape[0]
    t = lax.broadcasted_iota(I32, (P, buf), 0)
    r = lax.broadcasted_iota(I32, (P, buf), 1)
    diff = buf + t - r
    span = win_steps * dil
    ok = ((diff & (dil - 1)) == 0) & (diff <= span) & (diff >= 0)
    bias_steps = diff.astype(F32)
    t1 = lax.broadcasted_iota(I32, (P, 1), 0)
    scale = DIL_HD ** -0.5
    for hh in range(DIL_HPG):
        slope = _alibi_slope(group * DIL_HPG + hh)
        c = slice(hh * DIL_HD, (hh + 1) * DIL_HD)
        q = q_ref[:, c]
        head_rows = pl.ds(hh, buf, stride=H)
        s = lax.dot_general(q.astype(BF16), kc_ref[head_rows, :].astype(BF16), _NT, preferred_element_type=F32)
        s = jnp.where(ok, s * scale - slope * bias_steps, -jnp.inf)
        m = jnp.max(s, axis=-1, keepdims=True)
        s_new = []
        for rp in range(ts):
            d = t1 - rp
            ok_n = (d >= 0) & ((d & (dil - 1)) == 0) & (d <= span)
            new_row = slice(rp * H + hh, rp * H + hh + 1)
            sn = jnp.sum(q * kn_ref[new_row, :], axis=-1, keepdims=True) * scale - slope * d.astype(F32)
            sn = jnp.where(ok_n, sn, -jnp.inf)
            s_new.append(sn)
            m = jnp.maximum(m, sn)
        e = jnp.exp(s - m)
        e_new = [jnp.exp(sn - m) for sn in s_new]
        l = jnp.sum(e, axis=-1, keepdims=True)
        for en in e_new:
            l = l + en
        o = jnp.dot((e / l).astype(BF16), vc_ref[head_rows, :].astype(BF16), preferred_element_type=F32)
        for rp in range(ts):
            o = o + (e_new[rp] / l) * vn_ref[rp * H + hh:rp * H + hh + 1, :]
        o_ref[:, c] = o
        lse_ref[:, c] = jnp.broadcast_to(m + jnp.log(l), (P, DIL_HD))
    ok_ref[0:(buf - ts) * H, :] = kc_ref[ts * H:buf * H, :]
    ok_ref[(buf - ts) * H:buf * H, :] = kn_ref[...]
    ov_ref[0:(buf - ts) * H, :] = vc_ref[ts * H:buf * H, :]
    ov_ref[(buf - ts) * H:buf * H, :] = vn_ref[...]


def _dilated_sample(q8, kn, vn, cache_k, cache_v, layer, group, ts):
    win, dil = DIL_GROUPS[group]
    depth, ns, buf = cache_k.shape[:3]
    assert buf == win and buf % dil == 0, "cached window must hold exactly one full window"
    flat = lambda c: c.reshape(depth, ns, buf * DIL_HPG, DIL_HD)
    small = pl.BlockSpec((None, SAMPLE_PAD, DIL_W), lambda n: (n, 0, 0))
    new = pl.BlockSpec((None, ts * DIL_HPG, DIL_HD), lambda n: (n, 0, 0))
    big_in = pl.BlockSpec((None, None, buf * DIL_HPG, DIL_HD), lambda n: (layer, n, 0, 0))
    big_out = pl.BlockSpec((None, buf * DIL_HPG, DIL_HD), lambda n: (n, 0, 0))
    o, lse, k_out, v_out = pl.pallas_call(
        functools.partial(_dil_sample_kernel, group=group, dil=dil, win_steps=win // dil, ts=ts),
        grid=(ns,), in_specs=[small, new, new, big_in, big_in],
        out_specs=[small, small, big_out, big_out],
        out_shape=[jax.ShapeDtypeStruct((ns, SAMPLE_PAD, DIL_W), F32)] * 2
        + [jax.ShapeDtypeStruct((ns, buf * DIL_HPG, DIL_HD), F32)] * 2,
        compiler_params=_params(56), name=f"dilated_sample_g{group}")(q8, kn, vn, flat(cache_k), flat(cache_v))
    return o, lse, k_out.reshape(ns, buf, DIL_HPG, DIL_HD), v_out.reshape(ns, buf, DIL_HPG, DIL_HD)


def _combine_kernel(o1_ref, o2_ref, o3_ref, l1_ref, l2_ref, l3_ref, out_ref):
    l1, l2, l3 = l1_ref[...], l2_ref[...], l3_ref[...]
    m = jnp.maximum(jnp.maximum(l1, l2), l3)
    e1, e2, e3 = jnp.exp(l1 - m), jnp.exp(l2 - m), jnp.exp(l3 - m)
    z = e1 + e2 + e3
    out = (e1 / z) * o1_ref[...] + (e2 / z) * o2_ref[...] + (e3 / z) * o3_ref[...]
    out_ref[...] = out.astype(out_ref.dtype)


def _combine_groups(outs, lses):
    m, w = outs[0].shape
    tm = _pick(m, (1024, 512, 256, 128, 64, 8))
    spec = pl.BlockSpec((tm, w), lambda i: (i, 0))
    return pl.pallas_call(_combine_kernel, grid=(m // tm,), in_specs=[spec] * 6, out_specs=spec,
                          out_shape=jax.ShapeDtypeStruct((m, w), BF16),
                          compiler_params=_params(40), name="combine_groups")(*outs, *lses)


def _mem_attn_head(q, k, v):
    s = lax.dot_general(q.astype(BF16), k.astype(BF16), _NT, preferred_element_type=F32) * (MEM_HD ** -0.5)
    e = jnp.exp(s - jnp.max(s, axis=-1, keepdims=True))
    p = e / jnp.sum(e, axis=-1, keepdims=True)
    return jnp.dot(p.astype(BF16), v.astype(BF16), preferred_element_type=F32)


def _mem_attn_kernel(q_ref, k_ref, v_ref, o_ref):
    for h in range(MEM_HEADS):
        c = slice(h * MEM_HD, (h + 1) * MEM_HD)
        o_ref[:, c] = _mem_attn_head(q_ref[:, c], k_ref[:, c], v_ref[:, c]).astype(o_ref.dtype)


def _mem_attn_cached_kernel(q_ref, k_ref, v_ref, o_ref):
    chunks = MEM_HD // 128
    mem_len = k_ref.shape[0] // (MEM_HEADS * chunks)
    for h in range(MEM_HEADS):
        rows = [pl.ds(h * chunks + c, mem_len, stride=MEM_HEADS * chunks) for c in range(chunks)]
        cols = [slice(h * MEM_HD + c * 128, h * MEM_HD + (c + 1) * 128) for c in range(chunks)]
        s = sum(lax.dot_general(q_ref[:, cols[c]].astype(BF16), k_ref[rows[c], :].astype(BF16), _NT,
                                preferred_element_type=F32) for c in range(chunks)) * (MEM_HD ** -0.5)
        e = jnp.exp(s - jnp.max(s, axis=-1, keepdims=True))
        p = (e / jnp.sum(e, axis=-1, keepdims=True)).astype(BF16)
        for c in range(chunks):
            o_ref[:, cols[c]] = jnp.dot(p, v_ref[rows[c], :].astype(BF16),
                                        preferred_element_type=F32).astype(o_ref.dtype)


def _memory_attend_prompt(proj, mkv, batch, seq):
    mem_len = mkv.shape[1]
    tq = _pick(seq, (512, 256, 128))
    nq = seq // tq
    return pl.pallas_call(
        _mem_attn_kernel, grid=(batch, nq),
        in_specs=[pl.BlockSpec((tq, MEM_W), lambda b, i: (b * nq + i, OFF_MQ // MEM_W)),
                  pl.BlockSpec((None, mem_len, MEM_W), lambda b, i: (b, 0, 0)),
                  pl.BlockSpec((None, mem_len, MEM_W), lambda b, i: (b, 0, 1))],
        out_specs=pl.BlockSpec((tq, MEM_W), lambda b, i: (b * nq + i, 0)),
        out_shape=jax.ShapeDtypeStruct((batch * seq, MEM_W), BF16),
        compiler_params=_params(40), name="memory_attend_prompt")(proj, mkv, mkv)


def _memory_attend_sample(q8, mem_k, mem_v, layer):
    depth, ns, mem_len = mem_k.shape[:3]
    rows = mem_len * MEM_W // 128
    flat = lambda c: c.reshape(depth, ns, rows, 128)
    small = pl.BlockSpec((None, SAMPLE_PAD, MEM_W), lambda n: (n, 0, 0))
    big = pl.BlockSpec((None, None, rows, 128), lambda n: (layer, n, 0, 0))
    return pl.pallas_call(
        _mem_attn_cached_kernel, grid=(ns,), in_specs=[small, big, big], out_specs=small,
        out_shape=jax.ShapeDtypeStruct((ns, SAMPLE_PAD, MEM_W), F32),
        compiler_params=_params(24), name="memory_attend_sample")(q8, flat(mem_k), flat(mem_v))


def _mix_kernel(ar_ref, ad_ref, am_ref, wr_ref, wd_ref, wm_ref, gr_ref, gd_ref, gm_ref, o_ref):
    r = jnp.dot(ar_ref[...], wr_ref[...], preferred_element_type=F32)
    d = jnp.dot(ad_ref[...], wd_ref[...], preferred_element_type=F32)
    m = jnp.dot(am_ref[...], wm_ref[...], preferred_element_type=F32)
    mix = jax.nn.sigmoid(gr_ref[...]) * r + jax.nn.sigmoid(gd_ref[...]) * d + jax.nn.sigmoid(gm_ref[...]) * m
    o_ref[...] = mix.astype(o_ref.dtype)


def _branch_mix(a_r, a_d, a_m, w_r, w_d, w_m, proj):
    t = a_r.shape[0]
    tm = _pick(t, (832, 640, 512, 416, 320, 256, 128, 64, 8))
    tn = 512
    nj = D_MODEL // tn
    lhs = lambda w: pl.BlockSpec((tm, w), lambda i, j: (i, 0))
    rhs = lambda w: pl.BlockSpec((w, tn), lambda i, j: (0, j))
    gate = lambda b: pl.BlockSpec((tm, tn), lambda i, j: (i, OFF_GATE // tn + b * nj + j))
    return pl.pallas_call(
        _mix_kernel, grid=(t // tm, nj),
        in_specs=[lhs(RET_V), lhs(DIL_W), lhs(MEM_W), rhs(RET_V), rhs(DIL_W), rhs(MEM_W), gate(0), gate(1), gate(2)],
        out_specs=pl.BlockSpec((tm, tn), lambda i, j: (i, j)),
        out_shape=jax.ShapeDtypeStruct((t, D_MODEL), BF16),
        compiler_params=_params(56), name="branch_mix")(a_r, a_d, a_m, w_r, w_d, w_m, proj, proj, proj)


def _staircase(k):
    return [(a, k // (a + 1)) for a in range(k)]


def _peer_topk_kernel(q_ref, keys_ref, i_ref, j_ref, g_ref, s_scr, key_scr, cand_scr, pos_scr):
    tk = q_ref.shape[0]
    K = PEER_TOPK
    SUB = 8
    iota_k = lax.broadcasted_iota(I32, (K, tk), 0)

    def top_k_rows(score_refs, id_ref):
        def body(k, carry):
            sel = iota_k == k
            out = []
            for ref, (prev, top_s, top_i) in zip(score_refs, carry):
                best = best_id = None
                for v in range(ref.shape[0] // SUB):
                    rows = slice(v * SUB, (v + 1) * SUB)
                    ids = id_ref[rows, :]
                    sv = jnp.where(ids == prev, -jnp.inf, ref[rows, :])
                    ref[rows, :] = sv
                    if best is None:
                        best, best_id = sv, ids
                    else:
                        best_id = jnp.where(sv > best, ids, best_id)
                        best = jnp.maximum(best, sv)
                m = jnp.max(best, axis=0, keepdims=True)
                pos = jnp.min(jnp.where(best == m, best_id, jnp.iinfo(jnp.int32).max), axis=0, keepdims=True)
                out.append((pos, jnp.where(sel, m, top_s), jnp.where(sel, pos, top_i)))
            return tuple(out)

        init = tuple((jnp.full((1, tk), -1, I32), jnp.zeros((K, tk), F32), jnp.zeros((K, tk), I32))
                     for _ in score_refs)
        return [(top_s, top_i) for _, top_s, top_i in lax.fori_loop(0, K, body, init)]

    for c in range(2):
        qc = q_ref[:, c * PEER_NKEYS:(c + 1) * PEER_NKEYS].astype(BF16)
        s_scr[c] = lax.dot_general(keys_ref[0, c].astype(BF16), qc, _NT, preferred_element_type=F32)
    key_scr[...] = lax.broadcasted_iota(I32, key_scr.shape, 0)
    (s1, i1), (s2, i2) = top_k_rows([s_scr.at[0], s_scr.at[1]], key_scr)

    cand_scr[...] = jnp.full(cand_scr.shape, -jnp.inf, F32)
    pos_scr[...] = jnp.full(pos_scr.shape, K * K, I32)
    off = 0
    for a, nb in _staircase(K):
        cand_scr[off:off + nb, :] = s1[a:a + 1, :] + s2[0:nb, :]
        pos_scr[off:off + nb, :] = lax.broadcasted_iota(I32, (nb, tk), 0) + a * K
        off += nb
    (best_s, best_p), = top_k_rows([cand_scr], pos_scr)
    pa = jnp.right_shift(best_p, K.bit_length() - 1)
    pb = best_p - pa * K
    ei = jnp.zeros((K, tk), I32)
    ej = jnp.zeros((K, tk), I32)
    for a in range(K):
        ei = jnp.where(pa == a, i1[a:a + 1, :], ei)
        ej = jnp.where(pb == a, i2[a:a + 1, :], ej)
    e = jnp.exp(best_s - jnp.max(best_s, axis=0, keepdims=True))
    i_ref[...] = ei.astype(F32)
    j_ref[...] = ej.astype(F32)
    g_ref[...] = e / jnp.sum(e, axis=0, keepdims=True)


def _peer_route(q, sub_keys):
    t = q.shape[0]
    tk = _pick(t, (640, 512, 256, 128))
    n_cand = -(-sum(nb for _, nb in _staircase(PEER_TOPK)) // 8) * 8
    out_spec = pl.BlockSpec((PEER_TOPK, tk), lambda i, h: (h, i))
    return pl.pallas_call(
        _peer_topk_kernel, grid=(t // tk, PEER_HEADS),
        in_specs=[pl.BlockSpec((tk, PEER_DQ), lambda i, h: (i, h)),
                  pl.BlockSpec((1, 2, PEER_NKEYS, PEER_DQ // 2), lambda i, h: (h, 0, 0, 0))],
        out_specs=[out_spec] * 3,
        out_shape=[jax.ShapeDtypeStruct((PEER_HEADS * PEER_TOPK, t), F32)] * 3,
        scratch_shapes=[pltpu.VMEM((2, PEER_NKEYS, tk), F32), pltpu.VMEM((PEER_NKEYS, tk), I32),
                        pltpu.VMEM((n_cand, tk), F32), pltpu.VMEM((n_cand, tk), I32)],
        compiler_params=_params(24), name="peer_route")(q, sub_keys)


def _peer_wbuild_kernel(i_ref, j_ref, g_ref, w_ref, it_scr, jt_scr, gt_scr, wt_scr):
    tw = w_ref.shape[0]
    it_scr[...] = i_ref[...].T
    jt_scr[...] = j_ref[...].T
    gt_scr[...] = g_ref[...].T
    n = PEER_NKEYS
    iota_r = lax.broadcasted_iota(I32, (n, i_ref.shape[0]), 0).astype(F32)

    def body(t, carry):
        irow = it_scr[pl.ds(t, 1), :]
        jrow = jt_scr[pl.ds(t, 1), :]
        grow = gt_scr[pl.ds(t, 1), :]
        a_t = jnp.where(iota_r == irow, grow, 0.0).astype(BF16)
        b_t = jnp.where(iota_r == jrow, 1.0, 0.0).astype(BF16)
        w = lax.dot_general(a_t, b_t, _NT, preferred_element_type=F32)
        wt_scr[pl.ds(pl.multiple_of(t * W_ROW_PITCH, 8), n), :] = w
        return carry

    lax.fori_loop(0, tw, body, 0, unroll=8)
    for i in range(n):
        w_ref[:, i * n:(i + 1) * n] = wt_scr[pl.ds(i, tw, stride=W_ROW_PITCH), :].astype(BF16)


def _peer_wbuild(ei, ej, g):
    nsel, t = ei.shape
    tw = 128
    in_spec = pl.BlockSpec((nsel, tw), lambda i: (0, i))
    n = PEER_NKEYS
    return pl.pallas_call(
        _peer_wbuild_kernel, grid=(t // tw,), in_specs=[in_spec] * 3,
        out_specs=pl.BlockSpec((tw, n * n), lambda i: (i, 0)),
        out_shape=jax.ShapeDtypeStruct((t, n * n), BF16),
        scratch_shapes=[pltpu.VMEM((tw, nsel), F32)] * 3 + [pltpu.VMEM((tw * W_ROW_PITCH, n), F32)],
        compiler_params=_params(32), name="peer_wbuild")(ei, ej, g)


def _gelu_tanh(x):
    return x * (0.5 * (1.0 + jnp.tanh(math.sqrt(2.0 / math.pi) * (x + 0.044715 * (x * x * x)))))


def _peer_mlp_kernel(x_ref, u_ref, v_ref, w_ref, o_ref):
    e = pl.program_id(1)

    @pl.when(e == 0)
    def _():
        o_ref[...] = jnp.zeros_like(o_ref)

    half = u_ref.shape[0] // 2
    ps = []
    for c in range(2):
        rows = slice(c * half, (c + 1) * half)
        act = lax.dot_general(x_ref[...], u_ref[rows, :], _NT, preferred_element_type=F32)
        ps.append((w_ref[:, rows].astype(F32) * _gelu_tanh(act)).astype(BF16))
    p = jnp.concatenate(ps, axis=1)
    o_ref[...] = o_ref[...] + jnp.dot(p, v_ref[...], preferred_element_type=F32)


def _peer_mlp(n2, u, v, w):
    t, d = n2.shape
    ne = u.shape[0]
    tm = _pick(t, (416, 320, 256, 128, 64, 8))
    eb = 512
    return pl.pallas_call(
        _peer_mlp_kernel, grid=(t // tm, ne // eb),
        in_specs=[pl.BlockSpec((tm, d), lambda i, e: (i, 0)),
                  pl.BlockSpec((eb, d), lambda i, e: (e, 0)),
                  pl.BlockSpec((eb, d), lambda i, e: (e, 0)),
                  pl.BlockSpec((tm, eb), lambda i, e: (i, e))],
        out_specs=pl.BlockSpec((tm, d), lambda i, e: (i, 0)),
        out_shape=jax.ShapeDtypeStruct((t, d), F32),
        compiler_params=_params(60), name="peer_mlp")(n2, u, v, w)


def _pad_sample_rows(a, ns, ts):
    a = a.reshape(ns, ts, a.shape[-1])
    return jnp.pad(a, ((0, 0), (0, SAMPLE_PAD - ts), (0, 0)))


def kernel(x_prompt, x_sample, mem_prompt, cache_ret_state, cache_win_k1, cache_win_v1, cache_win_k2, cache_win_v2, cache_win_k3, cache_win_v3, cache_mem_k, cache_mem_v, norm1_w, w_in, ret_gn_w, w_ret_o, w_dil_o, mem_norm_w, w_mem_kv, w_mem_o, w_out, norm2_w, w_peer_q, peer_sub_keys, peer_u, peer_v, final_norm_w):
    batch, seq, d = x_prompt.shape
    ns, ts, _ = x_sample.shape
    mem_len = mem_prompt.shape[1]
    depth = norm1_w.shape[0]
    assert d == D_MODEL and seq % RET_CHUNK == 0 and ts <= SAMPLE_PAD
    tp, tsamp = batch * seq, ns * ts
    win_k = (cache_win_k1, cache_win_k2, cache_win_k3)
    win_v = (cache_win_v1, cache_win_v2, cache_win_v3)

    slopes = jnp.asarray([_alibi_slope(h) for h in range(DIL_HEADS)], F32)
    x_all = jnp.concatenate([x_prompt.reshape(tp, d), x_sample.reshape(tsamp, d)], axis=0)
    prompt_states, sample_states = [], []
    for l in range(depth):
        proj = _matmul(_rmsnorm(x_all, norm1_w[l], BF16), w_in[l],
                       tm_cands=(1664, 832, 640, 512, 416, 320, 256, 128, 64, 8), name="input_projection")
        proj_s = proj[tp:]

        ar_p, ret_p = _retention_prompt(proj, ret_gn_w[l], batch, seq)
        ar_s, ret_s = _retention_sample(proj_s, ret_gn_w[l], cache_ret_state[l], ts)

        outs_p, lses_p, outs_s, lses_s, bufs_p, bufs_s = [], [], [], [], [], []
        for g, (win, dil) in enumerate(DIL_GROUPS):
            o, lse = _dilated_prompt(proj, slopes, g, batch, seq)
            outs_p.append(o)
            lses_p.append(lse)
            keep = min(win, seq)
            for off in (OFF_DK, OFF_DV):
                c0 = off + g * DIL_W
                tail = [proj[b * seq + seq - keep:(b + 1) * seq, c0:c0 + DIL_W] for b in range(batch)]
                bufs_p.append(jnp.stack(tail, axis=0).reshape(batch, keep, DIL_HPG, DIL_HD))
            q_s, k_s, v_s = (proj_s[:, off + g * DIL_W: off + (g + 1) * DIL_W] for off in (OFF_DQ, OFF_DK, OFF_DV))
            o8, lse8, k_new, v_new = _dilated_sample(
                _pad_sample_rows(q_s, ns, ts), k_s.reshape(ns, ts * DIL_HPG, DIL_HD), v_s.reshape(ns, ts * DIL_HPG, DIL_HD),
                win_k[g], win_v[g], l, g, ts)
            outs_s.append(o8[:, :ts].reshape(tsamp, DIL_W))
            lses_s.append(lse8[:, :ts].reshape(tsamp, DIL_W))
            bufs_s += [k_new, v_new]
        ad_p = _combine_groups(outs_p, lses_p)
        ad_s = _combine_groups(outs_s, lses_s)

        mem_n = _rmsnorm(mem_prompt.reshape(batch * mem_len, d), mem_norm_w[l], BF16)
        mkv = _matmul(mem_n, w_mem_kv[l].astype(BF16), name="memory_kv").reshape(batch, mem_len, 2 * MEM_W)
        am_p = _memory_attend_prompt(proj, mkv, batch, seq)
        mq8 = _pad_sample_rows(proj_s[:, OFF_MQ:OFF_MQ + MEM_W], ns, ts)
        am_s = _memory_attend_sample(mq8, cache_mem_k, cache_mem_v, l)
        am_s = am_s[:, :ts].reshape(tsamp, MEM_W).astype(BF16)

        a_r = jnp.concatenate([ar_p, ar_s], axis=0)
        a_d = jnp.concatenate([ad_p, ad_s], axis=0)
        a_m = jnp.concatenate([am_p, am_s], axis=0)
        mix = _branch_mix(a_r, a_d, a_m, w_ret_o[l].astype(BF16), w_dil_o[l].astype(BF16),
                          w_mem_o[l].astype(BF16), proj)
        h = _matmul(mix, w_out[l].astype(BF16), res=x_all, name="output_projection")

        n2 = _rmsnorm(h, norm2_w[l], BF16)
        pq = _matmul(n2, w_peer_q[l].astype(BF16), name="peer_query")
        ei, ej, gw = _peer_route(pq, peer_sub_keys[l])
        w_dense = _peer_wbuild(ei, ej, gw)
        peer = _peer_mlp(n2, peer_u[l].astype(BF16), peer_v[l].astype(BF16), w_dense)

        mk = mkv[:, :, :MEM_W].reshape(batch, mem_len, MEM_HEADS, MEM_HD)
        mv = mkv[:, :, MEM_W:].reshape(batch, mem_len, MEM_HEADS, MEM_HD)
        prompt_states.append((ret_p, *bufs_p, mk, mv))
        sample_states.append((ret_s, *bufs_s))
        if l + 1 < depth:
            x_all = _add(h, peer)

    y_prompt = _add_rmsnorm(h, peer, final_norm_w, 0, tp).reshape(batch, seq, d)
    y_sample = _add_rmsnorm(h, peer, final_norm_w, tp, tsamp).reshape(ns, ts, d)
    p_out = [jnp.stack(s, axis=0) for s in zip(*prompt_states)]
    s_out = [jnp.stack(s, axis=0) for s in zip(*sample_states)]
    return (y_prompt, y_sample, *p_out, *s_out)
```

```python
import functools
import math

import jax
import jax.numpy as jnp
from jax import lax
from jax.experimental import pallas as pl
from jax.experimental.pallas import tpu as pltpu

F32, BF16, I32 = jnp.float32, jnp.bfloat16, jnp.int32

D_MODEL = 4096
RET_HEADS, RET_DK, RET_DV, RET_CHUNK = 8, 128, 256, 128
DIL_GROUPS = ((128, 1), (512, 4), (2048, 16))
DIL_HPG, DIL_HD, DIL_BLOCK = 4, 128, 128
DIL_HEADS = DIL_HPG * len(DIL_GROUPS)
ALIBI_MAX_EXP = 8.0
MEM_HEADS, MEM_HD = 4, 384
PEER_HEADS, PEER_NKEYS, PEER_TOPK, PEER_DQ = 8, 128, 16, 256
NORM_EPS = 1e-6

RET_QK = RET_HEADS * RET_DK
RET_V = RET_HEADS * RET_DV
DIL_W = DIL_HPG * DIL_HD
DIL_ALL = DIL_HEADS * DIL_HD
MEM_W = MEM_HEADS * MEM_HD
OFF_RQ = 0
OFF_RK = OFF_RQ + RET_QK
OFF_RV = OFF_RK + RET_QK
OFF_RG = OFF_RV + RET_V
OFF_DQ = OFF_RG + RET_V
OFF_DK = OFF_DQ + DIL_ALL
OFF_DV = OFF_DK + DIL_ALL
OFF_MQ = OFF_DV + DIL_ALL
OFF_GATE = OFF_MQ + MEM_W
PROJ_W = OFF_GATE + 3 * D_MODEL

SAMPLE_PAD = 8
W_ROW_PITCH = PEER_NKEYS + 8
V7X_VMEM_BYTES = 64 * 1024 * 1024
MIB = 1024 * 1024

_NT = (((1,), (1,)), ((), ()))


def _pick(n, cands):
    for c in cands:
        if n % c == 0:
            return c
    raise ValueError(f"no tile in {cands} divides {n}")


def _params(vmem_mib):
    return pltpu.CompilerParams(vmem_limit_bytes=min(vmem_mib * MIB, V7X_VMEM_BYTES - 2 * MIB))


def _rmsnorm_kernel(x_ref, w_ref, o_ref):
    x = x_ref[...]
    ms = jnp.mean(x * x, axis=-1, keepdims=True)
    o_ref[...] = (x * lax.rsqrt(ms + NORM_EPS) * w_ref[...]).astype(o_ref.dtype)


def _rmsnorm(x, w, out_dtype):
    m, d = x.shape
    tm = _pick(m, (320, 256, 128, 64, 8))
    return pl.pallas_call(
        _rmsnorm_kernel, grid=(m // tm,),
        in_specs=[pl.BlockSpec((tm, d), lambda i: (i, 0)), pl.BlockSpec((1, d), lambda i: (0, 0))],
        out_specs=pl.BlockSpec((tm, d), lambda i: (i, 0)),
        out_shape=jax.ShapeDtypeStruct((m, d), out_dtype),
        compiler_params=_params(48), name="rmsnorm")(x, w.reshape(1, d))


def _add_rmsnorm_kernel(a_ref, b_ref, w_ref, o_ref):
    x = a_ref[...] + b_ref[...]
    ms = jnp.mean(x * x, axis=-1, keepdims=True)
    o_ref[...] = x * lax.rsqrt(ms + NORM_EPS) * w_ref[...]


def _add_rmsnorm(a, b, w, row0, rows):
    d = a.shape[1]
    tm = _pick(math.gcd(row0, rows) if row0 else rows, (256, 128, 64, 8))
    blk0 = row0 // tm
    spec = pl.BlockSpec((tm, d), lambda i: (blk0 + i, 0))
    return pl.pallas_call(
        _add_rmsnorm_kernel, grid=(rows // tm,),
        in_specs=[spec, spec, pl.BlockSpec((1, d), lambda i: (0, 0))],
        out_specs=pl.BlockSpec((tm, d), lambda i: (i, 0)),
        out_shape=jax.ShapeDtypeStruct((rows, d), F32),
        compiler_params=_params(56), name="add_rmsnorm")(a, b, w.reshape(1, d))


def _add_kernel(a_ref, b_ref, o_ref):
    o_ref[...] = a_ref[...] + b_ref[...]


def _add(a, b):
    m, d = a.shape
    tm = _pick(m, (320, 256, 128, 64, 8))
    spec = pl.BlockSpec((tm, d), lambda i: (i, 0))
    return pl.pallas_call(_add_kernel, grid=(m // tm,), in_specs=[spec, spec], out_specs=spec,
                          out_shape=jax.ShapeDtypeStruct((m, d), F32),
                          compiler_params=_params(56), name="residual_add")(a, b)


def _mm_kernel(a_ref, b_ref, o_ref):
    o_ref[...] = jnp.dot(a_ref[...], b_ref[...].astype(a_ref.dtype), preferred_element_type=F32).astype(o_ref.dtype)


def _mm_res_kernel(a_ref, b_ref, r_ref, o_ref):
    o_ref[...] = r_ref[...] + jnp.dot(a_ref[...], b_ref[...], preferred_element_type=F32)


def _matmul(a, b, res=None, tm_cands=(832, 640, 512, 416, 320, 256, 128, 64, 8), tn=512, name="matmul"):
    m, k = a.shape
    n = b.shape[1]
    tm = _pick(m, tm_cands)
    tn = _pick(n, (tn, 256, 128))
    in_specs = [pl.BlockSpec((tm, k), lambda i, j: (i, 0)), pl.BlockSpec((k, tn), lambda i, j: (0, j))]
    args = [a, b]
    body = _mm_kernel
    if res is not None:
        in_specs.append(pl.BlockSpec((tm, tn), lambda i, j: (i, j)))
        args.append(res)
        body = _mm_res_kernel
    return pl.pallas_call(
        body, grid=(m // tm, n // tn), in_specs=in_specs,
        out_specs=pl.BlockSpec((tm, tn), lambda i, j: (i, j)),
        out_shape=jax.ShapeDtypeStruct((m, n), F32),
        compiler_params=_params(56), name=name)(*args)


def _ret_log_decay(h):
    return math.log(1.0 - 2.0 ** (-5.0 - h))


def _group_norm_gate(o, gnw, rg):
    mu = jnp.mean(o, axis=-1, keepdims=True)
    d = o - mu
    var = jnp.mean(d * d, axis=-1, keepdims=True)
    y = d * lax.rsqrt(var + NORM_EPS) * gnw
    return (y * (rg * jax.nn.sigmoid(rg))).astype(BF16)


def _ret_prompt_kernel(q_ref, k_ref, v_ref, rg_ref, gnw_ref, ar_ref, st_ref, state_scr):
    c = pl.program_id(1)
    L = RET_CHUNK

    @pl.when(c == 0)
    def _():
        state_scr[...] = jnp.zeros_like(state_scr)

    diff = (lax.broadcasted_iota(I32, (L, L), 0) - lax.broadcasted_iota(I32, (L, L), 1)).astype(F32)
    row_k = lax.broadcasted_iota(I32, (L, RET_DK), 0).astype(F32)
    row_v = lax.broadcasted_iota(I32, (L, RET_DV), 0).astype(F32)
    def first_matmuls(h):
        lg = _ret_log_decay(h)
        ck = slice(h * RET_DK, (h + 1) * RET_DK)
        q = q_ref[:, ck].astype(BF16)
        k = k_ref[:, ck] * (RET_DK ** -0.5)
        v = v_ref[:, h * RET_DV:(h + 1) * RET_DV].astype(BF16)
        s = lax.dot_general(q, k.astype(BF16), _NT, preferred_element_type=F32)
        st = state_scr[h]
        o_cross = jnp.dot(q, st.astype(BF16), preferred_element_type=F32)
        kd = (k * jnp.exp(lg * (L - 1.0 - row_k))).T.astype(BF16)
        state_scr[h] = st * math.exp(lg * L) + jnp.dot(kd, v, preferred_element_type=F32)
        return s, o_cross, v

    def decay_scores(h, s, o_cross, v):
        lg = _ret_log_decay(h)
        decay = jnp.where(diff >= 0, jnp.exp(lg * jnp.maximum(diff, 0.0)), 0.0)
        return (s * decay).astype(BF16), o_cross * jnp.exp(lg * (row_v + 1.0)), v

    def finish(h, inner, o_cross, v):
        cv = slice(h * RET_DV, (h + 1) * RET_DV)
        o = jnp.dot(inner, v, preferred_element_type=F32) + o_cross
        ar_ref[:, cv] = _group_norm_gate(o, gnw_ref[:, cv], rg_ref[:, cv])

    stage1, stage2 = {}, {}
    for step in range(RET_HEADS + 2):
        if step < RET_HEADS:
            stage1[step] = first_matmuls(step)
        if 0 <= step - 1 < RET_HEADS:
            stage2[step - 1] = decay_scores(step - 1, *stage1.pop(step - 1))
        if step - 2 >= 0:
            finish(step - 2, *stage2.pop(step - 2))

    @pl.when(c == pl.num_programs(1) - 1)
    def _():
        st_ref[0] = state_scr[...]


def _retention_prompt(proj, gnw, batch, seq):
    nc = seq // RET_CHUNK
    row = lambda n, c: n * nc + c
    return pl.pallas_call(
        _ret_prompt_kernel, grid=(batch, nc),
        in_specs=[pl.BlockSpec((RET_CHUNK, RET_QK), lambda n, c: (row(n, c), OFF_RQ // RET_QK)),
                  pl.BlockSpec((RET_CHUNK, RET_QK), lambda n, c: (row(n, c), OFF_RK // RET_QK)),
                  pl.BlockSpec((RET_CHUNK, RET_V), lambda n, c: (row(n, c), OFF_RV // RET_V)),
                  pl.BlockSpec((RET_CHUNK, RET_V), lambda n, c: (row(n, c), OFF_RG // RET_V)),
                  pl.BlockSpec((1, RET_V), lambda n, c: (0, 0))],
        out_specs=[pl.BlockSpec((RET_CHUNK, RET_V), lambda n, c: (row(n, c), 0)),
                   pl.BlockSpec((1, RET_HEADS, RET_DK, RET_DV), lambda n, c: (n, 0, 0, 0))],
        out_shape=[jax.ShapeDtypeStruct((batch * seq, RET_V), BF16),
                   jax.ShapeDtypeStruct((batch, RET_HEADS, RET_DK, RET_DV), F32)],
        scratch_shapes=[pltpu.VMEM((RET_HEADS, RET_DK, RET_DV), F32)],
        compiler_params=_params(32), name="retention_prompt")(proj, proj, proj, proj, gnw.reshape(1, RET_V))


def _ret_sample_kernel(q_ref, k_ref, v_ref, rg_ref, gnw_ref, st_in_ref, ar_ref, st_out_ref, o_scr, *, ts, ns_blk):
    i = pl.program_id(0)
    R = q_ref.shape[0]

    def sample_and_pos(shape, axis):
        r = lax.broadcasted_iota(I32, shape, axis).astype(F32)
        sid = jnp.floor((r + 0.5) * (1.0 / ts))
        return sid.astype(I32), r - sid * ts

    sid_r, t_r = sample_and_pos((R, R), 0)
    sid_c, t_c = sample_and_pos((R, R), 1)
    sid_k, t_k = sample_and_pos((R, RET_DK), 0)
    _, t_v = sample_and_pos((R, RET_DV), 0)
    for h in range(RET_HEADS):
        lg = _ret_log_decay(h)
        ck = slice(h * RET_DK, (h + 1) * RET_DK)
        cv = slice(h * RET_DV, (h + 1) * RET_DV)
        q = q_ref[:, ck]
        k = k_ref[:, ck] * (RET_DK ** -0.5)
        v = v_ref[:, cv].astype(BF16)

        @pl.when(i == 0)
        def _():
            s = lax.dot_general(q.astype(BF16), k.astype(BF16), _NT, preferred_element_type=F32)
            dt = t_r - t_c
            decay = jnp.where((sid_r == sid_c) & (dt >= 0), jnp.exp(lg * jnp.maximum(dt, 0.0)), 0.0)
            o_scr[:, cv] = jnp.dot((s * decay).astype(BF16), v, preferred_element_type=F32)

        q_dec = jnp.exp(lg * (t_v + 1.0))
        kd = k * jnp.exp(lg * (ts - 1.0 - t_k))
        for j in range(ns_blk):
            mine = sid_k == i * ns_blk + j
            st = st_in_ref[j, h]
            qm = jnp.where(mine, q, 0.0).astype(BF16)
            o_scr[:, cv] += jnp.dot(qm, st.astype(BF16), preferred_element_type=F32) * q_dec
            kdm = jnp.where(mine, kd, 0.0).T.astype(BF16)
            st_out_ref[j, h] = st * math.exp(lg * ts) + jnp.dot(kdm, v, preferred_element_type=F32)

    @pl.when(i == pl.num_programs(0) - 1)
    def _():
        for h in range(RET_HEADS):
            cv = slice(h * RET_DV, (h + 1) * RET_DV)
            ar_ref[:, cv] = _group_norm_gate(o_scr[:, cv], gnw_ref[:, cv], rg_ref[:, cv])


def _retention_sample(proj_s, gnw, state, ts):
    rows = proj_s.shape[0]
    ns = state.shape[0]
    ns_blk = _pick(ns, (4, 2, 1))
    return pl.pallas_call(
        functools.partial(_ret_sample_kernel, ts=ts, ns_blk=ns_blk), grid=(ns // ns_blk,),
        in_specs=[pl.BlockSpec((rows, RET_QK), lambda i: (0, OFF_RQ // RET_QK)),
                  pl.BlockSpec((rows, RET_QK), lambda i: (0, OFF_RK // RET_QK)),
                  pl.BlockSpec((rows, RET_V), lambda i: (0, OFF_RV // RET_V)),
                  pl.BlockSpec((rows, RET_V), lambda i: (0, OFF_RG // RET_V)),
                  pl.BlockSpec((1, RET_V), lambda i: (0, 0)),
                  pl.BlockSpec((ns_blk, RET_HEADS, RET_DK, RET_DV), lambda i: (i, 0, 0, 0))],
        out_specs=[pl.BlockSpec((rows, RET_V), lambda i: (0, 0)),
                   pl.BlockSpec((ns_blk, RET_HEADS, RET_DK, RET_DV), lambda i: (i, 0, 0, 0))],
        out_shape=[jax.ShapeDtypeStruct((rows, RET_V), BF16),
                   jax.ShapeDtypeStruct(state.shape, F32)],
        scratch_shapes=[pltpu.VMEM((rows, RET_V), F32)],
        compiler_params=_params(40), name="retention_sample")(
            proj_s, proj_s, proj_s, proj_s, gnw.reshape(1, RET_V), state)


def _alibi_slope(head):
    return 2.0 ** (-ALIBI_MAX_EXP * (head + 1.0) / DIL_HEADS)


def _dil_prompt_kernel(slope_ref, q_ref, k_ref, kp_ref, v_ref, vp_ref, o_ref, lse_ref, *, group, dil, win_steps, nq):
    j = pl.program_id(1)
    hh = pl.program_id(2)
    slope = slope_ref[group * DIL_HPG + hh]
    B = DIL_BLOCK
    qi = lax.broadcasted_iota(I32, (B, B), 0)
    kj = lax.broadcasted_iota(I32, (B, B), 1)
    steps_prev = qi + B - kj
    steps_cur = qi - kj
    in_prev = steps_prev <= win_steps
    in_prev_first = in_prev & (jnp.full((B, B), j, I32) > 0)
    in_cur = (steps_cur >= 0) & (steps_cur <= win_steps)
    bias_prev = slope * (steps_prev * dil).astype(F32)
    bias_cur = slope * (steps_cur * dil).astype(F32)
    scale = DIL_HD ** -0.5
    tiles = [(qb, r) for qb in range(nq) for r in range(dil)]

    def rows_of(qb, r):
        return pl.ds(qb * B * dil + r, B, stride=dil)

    def prev_of(ref, first_ref, qb, r):
        return first_ref[rows_of(0, r), :] if qb == 0 else ref[rows_of(qb - 1, r), :]

    def scores(qb, r):
        q = q_ref[rows_of(qb, r), :].astype(BF16)
        sp = lax.dot_general(q, prev_of(k_ref, kp_ref, qb, r).astype(BF16), _NT, preferred_element_type=F32)
        sc = lax.dot_general(q, k_ref[rows_of(qb, r), :].astype(BF16), _NT, preferred_element_type=F32)
        return sp, sc

    def softmax(qb, sp, sc):
        sp = jnp.where(in_prev_first if qb == 0 else in_prev, sp * scale - bias_prev, -jnp.inf)
        sc = jnp.where(in_cur, sc * scale - bias_cur, -jnp.inf)
        m = jnp.maximum(jnp.max(sp, axis=-1, keepdims=True), jnp.max(sc, axis=-1, keepdims=True))
        ep = jnp.exp(sp - m)
        ec = jnp.exp(sc - m)
        l = jnp.sum(ep, axis=-1, keepdims=True) + jnp.sum(ec, axis=-1, keepdims=True)
        return (ep / l).astype(BF16), (ec / l).astype(BF16), m + jnp.log(l)

    def finish(qb, r, pp, pc, lse):
        o = jnp.dot(pp, prev_of(v_ref, vp_ref, qb, r).astype(BF16), preferred_element_type=F32)
        o = o + jnp.dot(pc, v_ref[rows_of(qb, r), :].astype(BF16), preferred_element_type=F32)
        o_ref[rows_of(qb, r), :] = o
        lse_ref[rows_of(qb, r), :] = jnp.broadcast_to(lse, (B, DIL_HD))

    s_ready, p_ready = {}, {}
    lag_softmax, lag_finish = 2, 4
    for step in range(len(tiles) + lag_finish):
        if step < len(tiles):
            s_ready[step] = scores(*tiles[step])
        if 0 <= step - lag_softmax < len(tiles):
            p_ready[step - lag_softmax] = softmax(tiles[step - lag_softmax][0], *s_ready.pop(step - lag_softmax))
        if 0 <= step - lag_finish < len(tiles):
            finish(*tiles[step - lag_finish], *p_ready.pop(step - lag_finish))


def _dilated_prompt(proj, slopes, group, batch, seq):
    win, dil = DIL_GROUPS[group]
    span = DIL_BLOCK * dil
    assert seq % span == 0
    nq = _pick(seq // span, [n for n in (8, 4, 2, 1) if n * dil <= 16])
    nj = seq // (span * nq)
    col = lambda off: off // DIL_HD + group * DIL_HPG

    def cur(off):
        return pl.BlockSpec((span * nq, DIL_HD), lambda b, j, hh: (b * nj + j, col(off) + hh))

    def prev(off):
        return pl.BlockSpec((span, DIL_HD), lambda b, j, hh: (jnp.maximum((b * nj + j) * nq - 1, 0), col(off) + hh))

    out_spec = pl.BlockSpec((span * nq, DIL_HD), lambda b, j, hh: (b * nj + j, hh))
    return pl.pallas_call(
        functools.partial(_dil_prompt_kernel, group=group, dil=dil, win_steps=win // dil, nq=nq),
        grid=(batch, nj, DIL_HPG),
        in_specs=[pl.BlockSpec(memory_space=pltpu.SMEM), cur(OFF_DQ), cur(OFF_DK), prev(OFF_DK), cur(OFF_DV), prev(OFF_DV)],
        out_specs=[out_spec, out_spec],
        out_shape=[jax.ShapeDtypeStruct((batch * seq, DIL_W), F32)] * 2,
        compiler_params=_params(32), name=f"dilated_prompt_g{group}")(slopes, proj, proj, proj, proj, proj)


def _dil_sample_kernel(q_ref, kn_ref, vn_ref, kc_ref, vc_ref, o_ref, lse_ref, ok_ref, ov_ref,
                       *, group, dil, win_steps, ts):
    H = DIL_HPG
    buf = kc_ref.shape[0] // H
    P = q_ref.shape[0]
    t = lax.broadcasted_iota(I32, (P, buf), 0)
    r = lax.broadcasted_iota(I32, (P, buf), 1)
    diff = buf + t - r
    span = win_steps * dil
    ok = ((diff & (dil - 1)) == 0) & (diff <= span) & (diff >= 0)
    bias_steps = diff.astype(F32)
    t1 = lax.broadcasted_iota(I32, (P, 1), 0)
    scale = DIL_HD ** -0.5
    for hh in range(DIL_HPG):
        slope = _alibi_slope(group * DIL_HPG + hh)
        c = slice(hh * DIL_HD, (hh + 1) * DIL_HD)
        q = q_ref[:, c]
        head_rows = pl.ds(hh, buf, stride=H)
        s = lax.dot_general(q.astype(BF16), kc_ref[head_rows, :].astype(BF16), _NT, preferred_element_type=F32)
        s = jnp.where(ok, s * scale - slope * bias_steps, -jnp.inf)
        m = jnp.max(s, axis=-1, keepdims=True)
        s_new = []
        for rp in range(ts):
            d = t1 - rp
            ok_n = (d >= 0) & ((d & (dil - 1)) == 0) & (d <= span)
            new_row = slice(rp * H + hh, rp * H + hh + 1)
            sn = jnp.sum(q * kn_ref[new_row, :], axis=-1, keepdims=True) * scale - slope * d.astype(F32)
            sn = jnp.where(ok_n, sn, -jnp.inf)
            s_new.append(sn)
            m = jnp.maximum(m, sn)
        e = jnp.exp(s - m)
        e_new = [jnp.exp(sn - m) for sn in s_new]
        l = jnp.sum(e, axis=-1, keepdims=True)
        for en in e_new:
            l = l + en
        o = jnp.dot((e / l).astype(BF16), vc_ref[head_rows, :].astype(BF16), preferred_element_type=F32)
        for rp in range(ts):
            o = o + (e_new[rp] / l) * vn_ref[rp * H + hh:rp * H + hh + 1, :]
        o_ref[:, c] = o
        lse_ref[:, c] = jnp.broadcast_to(m + jnp.log(l), (P, DIL_HD))
    ok_ref[0:(buf - ts) * H, :] = kc_ref[ts * H:buf * H, :]
    ok_ref[(buf - ts) * H:buf * H, :] = kn_ref[...]
    ov_ref[0:(buf - ts) * H, :] = vc_ref[ts * H:buf * H, :]
    ov_ref[(buf - ts) * H:buf * H, :] = vn_ref[...]


def _dilated_sample(q8, kn, vn, cache_k, cache_v, layer, group, ts):
    win, dil = DIL_GROUPS[group]
    depth, ns, buf = cache_k.shape[:3]
    assert buf == win and buf % dil == 0, "cached window must hold exactly one full window"
    flat = lambda c: c.reshape(depth, ns, buf * DIL_HPG, DIL_HD)
    small = pl.BlockSpec((None, SAMPLE_PAD, DIL_W), lambda n: (n, 0, 0))
    new = pl.BlockSpec((None, ts * DIL_HPG, DIL_HD), lambda n: (n, 0, 0))
    big_in = pl.BlockSpec((None, None, buf * DIL_HPG, DIL_HD), lambda n: (layer, n, 0, 0))
    big_out = pl.BlockSpec((None, buf * DIL_HPG, DIL_HD), lambda n: (n, 0, 0))
    o, lse, k_out, v_out = pl.pallas_call(
        functools.partial(_dil_sample_kernel, group=group, dil=dil, win_steps=win // dil, ts=ts),
        grid=(ns,), in_specs=[small, new, new, big_in, big_in],
        out_specs=[small, small, big_out, big_out],
        out_shape=[jax.ShapeDtypeStruct((ns, SAMPLE_PAD, DIL_W), F32)] * 2
        + [jax.ShapeDtypeStruct((ns, buf * DIL_HPG, DIL_HD), F32)] * 2,
        compiler_params=_params(56), name=f"dilated_sample_g{group}")(q8, kn, vn, flat(cache_k), flat(cache_v))
    return o, lse, k_out.reshape(ns, buf, DIL_HPG, DIL_HD), v_out.reshape(ns, buf, DIL_HPG, DIL_HD)


def _combine_kernel(o1_ref, o2_ref, o3_ref, l1_ref, l2_ref, l3_ref, out_ref):
    l1, l2, l3 = l1_ref[...], l2_ref[...], l3_ref[...]
    m = jnp.maximum(jnp.maximum(l1, l2), l3)
    e1, e2, e3 = jnp.exp(l1 - m), jnp.exp(l2 - m), jnp.exp(l3 - m)
    z = e1 + e2 + e3
    out = (e1 / z) * o1_ref[...] + (e2 / z) * o2_ref[...] + (e3 / z) * o3_ref[...]
    out_ref[...] = out.astype(out_ref.dtype)


def _combine_groups(outs, lses):
    m, w = outs[0].shape
    tm = _pick(m, (1024, 512, 256, 128, 64, 8))
    spec = pl.BlockSpec((tm, w), lambda i: (i, 0))
    return pl.pallas_call(_combine_kernel, grid=(m // tm,), in_specs=[spec] * 6, out_specs=spec,
                          out_shape=jax.ShapeDtypeStruct((m, w), BF16),
                          compiler_params=_params(40), name="combine_groups")(*outs, *lses)


def _mem_attn_head(q, k, v):
    s = lax.dot_general(q.astype(BF16), k.astype(BF16), _NT, preferred_element_type=F32) * (MEM_HD ** -0.5)
    e = jnp.exp(s - jnp.max(s, axis=-1, keepdims=True))
    p = e / jnp.sum(e, axis=-1, keepdims=True)
    return jnp.dot(p.astype(BF16), v.astype(BF16), preferred_element_type=F32)


def _mem_attn_kernel(q_ref, k_ref, v_ref, o_ref):
    for h in range(MEM_HEADS):
        c = slice(h * MEM_HD, (h + 1) * MEM_HD)
        o_ref[:, c] = _mem_attn_head(q_ref[:, c], k_ref[:, c], v_ref[:, c]).astype(o_ref.dtype)


def _mem_attn_cached_kernel(q_ref, k_ref, v_ref, o_ref):
    chunks = MEM_HD // 128
    mem_len = k_ref.shape[0] // (MEM_HEADS * chunks)
    for h in range(MEM_HEADS):
        rows = [pl.ds(c * MEM_HEADS + h, mem_len, stride=MEM_HEADS * chunks) for c in range(chunks)]
        cols = [slice(h * MEM_HD + c * 128, h * MEM_HD + (c + 1) * 128) for c in range(chunks)]
        s = sum(lax.dot_general(q_ref[:, cols[c]].astype(BF16), k_ref[rows[c], :].astype(BF16), _NT,
                                preferred_element_type=F32) for c in range(chunks)) * (MEM_HD ** -0.5)
        e = jnp.exp(s - jnp.max(s, axis=-1, keepdims=True))
        p = (e / jnp.sum(e, axis=-1, keepdims=True)).astype(BF16)
        for c in range(chunks):
            o_ref[:, cols[c]] = jnp.dot(p, v_ref[rows[c], :].astype(BF16),
                                        preferred_element_type=F32).astype(o_ref.dtype)


def _memory_attend_prompt(proj, mkv, batch, seq):
    mem_len = mkv.shape[1]
    tq = _pick(seq, (512, 256, 128))
    nq = seq // tq
    return pl.pallas_call(
        _mem_attn_kernel, grid=(batch, nq),
        in_specs=[pl.BlockSpec((tq, MEM_W), lambda b, i: (b * nq + i, OFF_MQ // MEM_W)),
                  pl.BlockSpec((None, mem_len, MEM_W), lambda b, i: (b, 0, 0)),
                  pl.BlockSpec((None, mem_len, MEM_W), lambda b, i: (b, 0, 1))],
        out_specs=pl.BlockSpec((tq, MEM_W), lambda b, i: (b * nq + i, 0)),
        out_shape=jax.ShapeDtypeStruct((batch * seq, MEM_W), BF16),
        compiler_params=_params(40), name="memory_attend_prompt")(proj, mkv, mkv)


def _memory_attend_sample(q8, mem_k, mem_v, layer):
    depth, ns, mem_len = mem_k.shape[:3]
    rows = mem_len * MEM_W // 128
    chunks = MEM_HD // 128
    flat = lambda c: (c.reshape(depth, ns, mem_len, MEM_HEADS, chunks, 128)
                      .transpose(0, 1, 2, 4, 3, 5).reshape(depth, ns, rows, 128))
    small = pl.BlockSpec((None, SAMPLE_PAD, MEM_W), lambda n: (n, 0, 0))
    big = pl.BlockSpec((None, None, rows, 128), lambda n: (layer, n, 0, 0))
    return pl.pallas_call(
        _mem_attn_cached_kernel, grid=(ns,), in_specs=[small, big, big], out_specs=small,
        out_shape=jax.ShapeDtypeStruct((ns, SAMPLE_PAD, MEM_W), F32),
        compiler_params=_params(24), name="memory_attend_sample")(q8, flat(mem_k), flat(mem_v))


def _mix_kernel(ar_ref, ad_ref, am_ref, wr_ref, wd_ref, wm_ref, gr_ref, gd_ref, gm_ref, o_ref):
    r = jnp.dot(ar_ref[...], wr_ref[...], preferred_element_type=F32)
    d = jnp.dot(ad_ref[...], wd_ref[...], preferred_element_type=F32)
    m = jnp.dot(am_ref[...], wm_ref[...], preferred_element_type=F32)
    mix = jax.nn.sigmoid(gr_ref[...]) * r + jax.nn.sigmoid(gd_ref[...]) * d + jax.nn.sigmoid(gm_ref[...]) * m
    o_ref[...] = mix.astype(o_ref.dtype)


def _branch_mix(a_r, a_d, a_m, w_r, w_d, w_m, proj):
    t = a_r.shape[0]
    tm = _pick(t, (832, 640, 512, 416, 320, 256, 128, 64, 8))
    tn = 512
    nj = D_MODEL // tn
    lhs = lambda w: pl.BlockSpec((tm, w), lambda i, j: (i, 0))
    rhs = lambda w: pl.BlockSpec((w, tn), lambda i, j: (0, j))
    gate = lambda b: pl.BlockSpec((tm, tn), lambda i, j: (i, OFF_GATE // tn + b * nj + j))
    return pl.pallas_call(
        _mix_kernel, grid=(t // tm, nj),
        in_specs=[lhs(RET_V), lhs(DIL_W), lhs(MEM_W), rhs(RET_V), rhs(DIL_W), rhs(MEM_W), gate(0), gate(1), gate(2)],
        out_specs=pl.BlockSpec((tm, tn), lambda i, j: (i, j)),
        out_shape=jax.ShapeDtypeStruct((t, D_MODEL), BF16),
        compiler_params=_params(56), name="branch_mix")(a_r, a_d, a_m, w_r, w_d, w_m, proj, proj, proj)


def _staircase(k):
    return [(a, k // (a + 1)) for a in range(k)]


def _peer_topk_kernel(q_ref, keys_ref, i_ref, j_ref, g_ref, s_scr, key_scr, cand_scr, pos_scr):
    tk = q_ref.shape[0]
    K = PEER_TOPK
    SUB = 8
    iota_k = lax.broadcasted_iota(I32, (K, tk), 0)

    def top_k_rows(score_refs, id_ref):
        def body(k, carry):
            sel = iota_k == k
            out = []
            for ref, (prev, top_s, top_i) in zip(score_refs, carry):
                best = best_id = None
                for v in range(ref.shape[0] // SUB):
                    rows = slice(v * SUB, (v + 1) * SUB)
                    ids = id_ref[rows, :]
                    sv = jnp.where(ids == prev, -jnp.inf, ref[rows, :])
                    ref[rows, :] = sv
                    if best is None:
                        best, best_id = sv, ids
                    else:
                        best_id = jnp.where(sv > best, ids, best_id)
                        best = jnp.maximum(best, sv)
                m = jnp.max(best, axis=0, keepdims=True)
                pos = jnp.min(jnp.where(best == m, best_id, jnp.iinfo(jnp.int32).max), axis=0, keepdims=True)
                out.append((pos, jnp.where(sel, m, top_s), jnp.where(sel, pos, top_i)))
            return tuple(out)

        init = tuple((jnp.full((1, tk), -1, I32), jnp.zeros((K, tk), F32), jnp.zeros((K, tk), I32))
                     for _ in score_refs)
        return [(top_s, top_i) for _, top_s, top_i in lax.fori_loop(0, K, body, init)]

    for c in range(2):
        qc = q_ref[:, c * PEER_NKEYS:(c + 1) * PEER_NKEYS].astype(BF16)
        s_scr[c] = lax.dot_general(keys_ref[0, c].astype(BF16), qc, _NT, preferred_element_type=F32)
    key_scr[...] = lax.broadcasted_iota(I32, key_scr.shape, 0)
    (s1, i1), (s2, i2) = top_k_rows([s_scr.at[0], s_scr.at[1]], key_scr)

    cand_scr[...] = jnp.full(cand_scr.shape, -jnp.inf, F32)
    pos_scr[...] = jnp.full(pos_scr.shape, K * K, I32)
    off = 0
    for a, nb in _staircase(K):
        cand_scr[off:off + nb, :] = s1[a:a + 1, :] + s2[0:nb, :]
        pos_scr[off:off + nb, :] = lax.broadcasted_iota(I32, (nb, tk), 0) + a * K
        off += nb
    (best_s, best_p), = top_k_rows([cand_scr], pos_scr)
    pa = jnp.right_shift(best_p, K.bit_length() - 1)
    pb = best_p - pa * K
    ei = jnp.zeros((K, tk), I32)
    ej = jnp.zeros((K, tk), I32)
    for a in range(K):
        ei = jnp.where(pa == a, i1[a:a + 1, :], ei)
        ej = jnp.where(pb == a, i2[a:a + 1, :], ej)
    e = jnp.exp(best_s - jnp.max(best_s, axis=0, keepdims=True))
    i_ref[...] = ei.astype(F32)
    j_ref[...] = ej.astype(F32)
    g_ref[...] = e / jnp.sum(e, axis=0, keepdims=True)


def _peer_route(q, sub_keys):
    t = q.shape[0]
    tk = _pick(t, (640, 512, 256, 128))
    n_cand = -(-sum(nb for _, nb in _staircase(PEER_TOPK)) // 8) * 8
    out_spec = pl.BlockSpec((PEER_TOPK, tk), lambda i, h: (h, i))
    return pl.pallas_call(
        _peer_topk_kernel, grid=(t // tk, PEER_HEADS),
        in_specs=[pl.BlockSpec((tk, PEER_DQ), lambda i, h: (i, h)),
                  pl.BlockSpec((1, 2, PEER_NKEYS, PEER_DQ // 2), lambda i, h: (h, 0, 0, 0))],
        out_specs=[out_spec] * 3,
        out_shape=[jax.ShapeDtypeStruct((PEER_HEADS * PEER_TOPK, t), F32)] * 3,
        scratch_shapes=[pltpu.VMEM((2, PEER_NKEYS, tk), F32), pltpu.VMEM((PEER_NKEYS, tk), I32),
                        pltpu.VMEM((n_cand, tk), F32), pltpu.VMEM((n_cand, tk), I32)],
        compiler_params=_params(24), name="peer_route")(q, sub_keys)


def _peer_wbuild_kernel(i_ref, j_ref, g_ref, w_ref, it_scr, jt_scr, gt_scr, wt_scr):
    tw = w_ref.shape[0]
    it_scr[...] = i_ref[...].T
    jt_scr[...] = j_ref[...].T
    gt_scr[...] = g_ref[...].T
    n = PEER_NKEYS
    iota_r = lax.broadcasted_iota(I32, (n, i_ref.shape[0]), 0).astype(F32)

    def body(t, carry):
        irow = it_scr[pl.ds(t, 1), :]
        jrow = jt_scr[pl.ds(t, 1), :]
        grow = gt_scr[pl.ds(t, 1), :]
        a_t = jnp.where(iota_r == irow, grow, 0.0).astype(BF16)
        b_t = jnp.where(iota_r == jrow, 1.0, 0.0).astype(BF16)
        w = lax.dot_general(a_t, b_t, _NT, preferred_element_type=F32)
        wt_scr[pl.ds(pl.multiple_of(t * W_ROW_PITCH, 8), n), :] = w
        return carry

    lax.fori_loop(0, tw, body, 0, unroll=8)
    for i in range(n):
        w_ref[:, i * n:(i + 1) * n] = wt_scr[pl.ds(i, tw, stride=W_ROW_PITCH), :].astype(BF16)


def _peer_wbuild(ei, ej, g):
    nsel, t = ei.shape
    tw = 128
    in_spec = pl.BlockSpec((nsel, tw), lambda i: (0, i))
    n = PEER_NKEYS
    return pl.pallas_call(
        _peer_wbuild_kernel, grid=(t // tw,), in_specs=[in_spec] * 3,
        out_specs=pl.BlockSpec((tw, n * n), lambda i: (i, 0)),
        out_shape=jax.ShapeDtypeStruct((t, n * n), BF16),
        scratch_shapes=[pltpu.VMEM((tw, nsel), F32)] * 3 + [pltpu.VMEM((tw * W_ROW_PITCH, n), F32)],
        compiler_params=_params(32), name="peer_wbuild")(ei, ej, g)


def _gelu_tanh(x):
    return x * (0.5 * (1.0 + jnp.tanh(math.sqrt(2.0 / math.pi) * (x + 0.044715 * (x * x * x)))))


def _peer_mlp_kernel(x_ref, u_ref, v_ref, w_ref, o_ref):
    e = pl.program_id(1)

    @pl.when(e == 0)
    def _():
        o_ref[...] = jnp.zeros_like(o_ref)

    half = u_ref.shape[0] // 2
    ps = []
    for c in range(2):
        rows = slice(c * half, (c + 1) * half)
        act = lax.dot_general(x_ref[...], u_ref[rows, :], _NT, preferred_element_type=F32)
        ps.append((w_ref[:, rows].astype(F32) * _gelu_tanh(act)).astype(BF16))
    p = jnp.concatenate(ps, axis=1)
    o_ref[...] = o_ref[...] + jnp.dot(p, v_ref[...], preferred_element_type=F32)


def _peer_mlp(n2, u, v, w):
    t, d = n2.shape
    ne = u.shape[0]
    tm = _pick(t, (640, 416, 320, 256, 128, 64, 8))
    eb = 512
    return pl.pallas_call(
        _peer_mlp_kernel, grid=(t // tm, ne // eb),
        in_specs=[pl.BlockSpec((tm, d), lambda i, e: (i, 0)),
                  pl.BlockSpec((eb, d), lambda i, e: (e, 0)),
                  pl.BlockSpec((eb, d), lambda i, e: (e, 0)),
                  pl.BlockSpec((tm, eb), lambda i, e: (i, e))],
        out_specs=pl.BlockSpec((tm, d), lambda i, e: (i, 0)),
        out_shape=jax.ShapeDtypeStruct((t, d), F32),
        compiler_params=_params(62), name="peer_mlp")(n2, u, v, w)


def _pad_sample_rows(a, ns, ts):
    a = a.reshape(ns, ts, a.shape[-1])
    return jnp.pad(a, ((0, 0), (0, SAMPLE_PAD - ts), (0, 0)))


def kernel(x_prompt, x_sample, mem_prompt, cache_ret_state, cache_win_k1, cache_win_v1, cache_win_k2, cache_win_v2, cache_win_k3, cache_win_v3, cache_mem_k, cache_mem_v, norm1_w, w_in, ret_gn_w, w_ret_o, w_dil_o, mem_norm_w, w_mem_kv, w_mem_o, w_out, norm2_w, w_peer_q, peer_sub_keys, peer_u, peer_v, final_norm_w):
    batch, seq, d = x_prompt.shape
    ns, ts, _ = x_sample.shape
    mem_len = mem_prompt.shape[1]
    depth = norm1_w.shape[0]
    assert d == D_MODEL and seq % RET_CHUNK == 0 and ts <= SAMPLE_PAD
    tp, tsamp = batch * seq, ns * ts
    win_k = (cache_win_k1, cache_win_k2, cache_win_k3)
    win_v = (cache_win_v1, cache_win_v2, cache_win_v3)

    slopes = jnp.asarray([_alibi_slope(h) for h in range(DIL_HEADS)], F32)
    x_all = jnp.concatenate([x_prompt.reshape(tp, d), x_sample.reshape(tsamp, d)], axis=0)
    prompt_states, sample_states = [], []
    for l in range(depth):
        proj = _matmul(_rmsnorm(x_all, norm1_w[l], BF16), w_in[l],
                       tm_cands=(1664, 832, 640, 512, 416, 320, 256, 128, 64, 8), name="input_projection")
        proj_s = proj[tp:]

        ar_p, ret_p = _retention_prompt(proj, ret_gn_w[l], batch, seq)
        ar_s, ret_s = _retention_sample(proj_s, ret_gn_w[l], cache_ret_state[l], ts)

        outs_p, lses_p, outs_s, lses_s, bufs_p, bufs_s = [], [], [], [], [], []
        for g, (win, dil) in enumerate(DIL_GROUPS):
            o, lse = _dilated_prompt(proj, slopes, g, batch, seq)
            outs_p.append(o)
            lses_p.append(lse)
            keep = min(win, seq)
            for off in (OFF_DK, OFF_DV):
                c0 = off + g * DIL_W
                tail = [proj[b * seq + seq - keep:(b + 1) * seq, c0:c0 + DIL_W] for b in range(batch)]
                bufs_p.append(jnp.stack(tail, axis=0).reshape(batch, keep, DIL_HPG, DIL_HD))
            q_s, k_s, v_s = (proj_s[:, off + g * DIL_W: off + (g + 1) * DIL_W] for off in (OFF_DQ, OFF_DK, OFF_DV))
            o8, lse8, k_new, v_new = _dilated_sample(
                _pad_sample_rows(q_s, ns, ts), k_s.reshape(ns, ts * DIL_HPG, DIL_HD), v_s.reshape(ns, ts * DIL_HPG, DIL_HD),
                win_k[g], win_v[g], l, g, ts)
            outs_s.append(o8[:, :ts].reshape(tsamp, DIL_W))
            lses_s.append(lse8[:, :ts].reshape(tsamp, DIL_W))
            bufs_s += [k_new, v_new]
        ad_p = _combine_groups(outs_p, lses_p)
        ad_s = _combine_groups(outs_s, lses_s)

        mem_n = _rmsnorm(mem_prompt.reshape(batch * mem_len, d), mem_norm_w[l], BF16)
        mkv = _matmul(mem_n, w_mem_kv[l].astype(BF16), name="memory_kv").reshape(batch, mem_len, 2 * MEM_W)
        am_p = _memory_attend_prompt(proj, mkv, batch, seq)
        mq8 = _pad_sample_rows(proj_s[:, OFF_MQ:OFF_MQ + MEM_W], ns, ts)
        am_s = _memory_attend_sample(mq8, cache_mem_k, cache_mem_v, l)
        am_s = am_s[:, :ts].reshape(tsamp, MEM_W).astype(BF16)

        a_r = jnp.concatenate([ar_p, ar_s], axis=0)
        a_d = jnp.concatenate([ad_p, ad_s], axis=0)
        a_m = jnp.concatenate([am_p, am_s], axis=0)
        mix = _branch_mix(a_r, a_d, a_m, w_ret_o[l].astype(BF16), w_dil_o[l].astype(BF16),
                          w_mem_o[l].astype(BF16), proj)
        h = _matmul(mix, w_out[l].astype(BF16), res=x_all, name="output_projection")

        n2 = _rmsnorm(h, norm2_w[l], BF16)
        pq = _matmul(n2, w_peer_q[l].astype(BF16), name="peer_query")
        ei, ej, gw = _peer_route(pq, peer_sub_keys[l])
        w_dense = _peer_wbuild(ei, ej, gw)
        peer = _peer_mlp(n2, peer_u[l].astype(BF16), peer_v[l].astype(BF16), w_dense)

        mk = mkv[:, :, :MEM_W].reshape(batch, mem_len, MEM_HEADS, MEM_HD)
        mv = mkv[:, :, MEM_W:].reshape(batch, mem_len, MEM_HEADS, MEM_HD)
        prompt_states.append((ret_p, *bufs_p, mk, mv))
        sample_states.append((ret_s, *bufs_s))
        if l + 1 < depth:
            x_all = _add(h, peer)

    y_prompt = _add_rmsnorm(h, peer, final_norm_w, 0, tp).reshape(batch, seq, d)
    y_sample = _add_rmsnorm(h, peer, final_norm_w, tp, tsamp).reshape(ns, ts, d)
    p_out = [jnp.stack(s, axis=0) for s in zip(*prompt_states)]
    s_out = [jnp.stack(s, axis=0) for s in zip(*sample_states)]
    return (y_prompt, y_sample, *p_out, *s_out)
```

```python
import functools
import math

import jax
import jax.numpy as jnp
from jax import lax
from jax.experimental import pallas as pl
from jax.experimental.pallas import tpu as pltpu

F32, BF16, I32 = jnp.float32, jnp.bfloat16, jnp.int32

D_MODEL = 4096
RET_HEADS, RET_DK, RET_DV, RET_CHUNK = 8, 128, 256, 128
DIL_GROUPS = ((128, 1), (512, 4), (2048, 16))
DIL_HPG, DIL_HD, DIL_BLOCK = 4, 128, 128
DIL_HEADS = DIL_HPG * len(DIL_GROUPS)
ALIBI_MAX_EXP = 8.0
MEM_HEADS, MEM_HD = 4, 384
PEER_HEADS, PEER_NKEYS, PEER_TOPK, PEER_DQ = 8, 128, 16, 256
NORM_EPS = 1e-6

RET_QK = RET_HEADS * RET_DK
RET_V = RET_HEADS * RET_DV
DIL_W = DIL_HPG * DIL_HD
DIL_ALL = DIL_HEADS * DIL_HD
MEM_W = MEM_HEADS * MEM_HD
OFF_RQ = 0
OFF_RK = OFF_RQ + RET_QK
OFF_RV = OFF_RK + RET_QK
OFF_RG = OFF_RV + RET_V
OFF_DQ = OFF_RG + RET_V
OFF_DK = OFF_DQ + DIL_ALL
OFF_DV = OFF_DK + DIL_ALL
OFF_MQ = OFF_DV + DIL_ALL
OFF_GATE = OFF_MQ + MEM_W
PROJ_W = OFF_GATE + 3 * D_MODEL

SAMPLE_PAD = 8
W_ROW_PITCH = PEER_NKEYS + 8
V7X_VMEM_BYTES = 64 * 1024 * 1024
MIB = 1024 * 1024

_NT = (((1,), (1,)), ((), ()))


def _pick(n, cands):
    for c in cands:
        if n % c == 0:
            return c
    raise ValueError(f"no tile in {cands} divides {n}")


def _params(vmem_mib):
    return pltpu.CompilerParams(vmem_limit_bytes=min(vmem_mib * MIB, V7X_VMEM_BYTES - 2 * MIB))


def _rmsnorm_kernel(x_ref, w_ref, o_ref):
    x = x_ref[...]
    ms = jnp.mean(x * x, axis=-1, keepdims=True)
    o_ref[...] = (x * lax.rsqrt(ms + NORM_EPS) * w_ref[...]).astype(o_ref.dtype)


def _rmsnorm(x, w, out_dtype):
    m, d = x.shape
    tm = _pick(m, (320, 256, 128, 64, 8))
    return pl.pallas_call(
        _rmsnorm_kernel, grid=(m // tm,),
        in_specs=[pl.BlockSpec((tm, d), lambda i: (i, 0)), pl.BlockSpec((1, d), lambda i: (0, 0))],
        out_specs=pl.BlockSpec((tm, d), lambda i: (i, 0)),
        out_shape=jax.ShapeDtypeStruct((m, d), out_dtype),
        compiler_params=_params(48), name="rmsnorm")(x, w.reshape(1, d))


def _add_rmsnorm_kernel(a_ref, b_ref, w_ref, o_ref):
    x = a_ref[...] + b_ref[...]
    ms = jnp.mean(x * x, axis=-1, keepdims=True)
    o_ref[...] = x * lax.rsqrt(ms + NORM_EPS) * w_ref[...]


def _rmsnorm_pair_kernel(a_ref, b_ref, w_ref, o_ref, *, na):
    i = pl.program_id(0)

    def norm(x):
        ms = jnp.mean(x * x, axis=-1, keepdims=True)
        return (x * lax.rsqrt(ms + NORM_EPS) * w_ref[...]).astype(o_ref.dtype)

    @pl.when(i < na)
    def _():
        o_ref[...] = norm(a_ref[...])

    @pl.when(i >= na)
    def _():
        o_ref[...] = norm(b_ref[...])


def _rmsnorm_pair(a, b, w, out_dtype):
    d = a.shape[1]
    tm = _pick(math.gcd(a.shape[0], b.shape[0]), (256, 128, 64, 8))
    na, nb = a.shape[0] // tm, b.shape[0] // tm
    return pl.pallas_call(
        functools.partial(_rmsnorm_pair_kernel, na=na), grid=(na + nb,),
        in_specs=[pl.BlockSpec((tm, d), lambda i: (jnp.minimum(i, na - 1), 0)),
                  pl.BlockSpec((tm, d), lambda i: (jnp.maximum(i - na, 0), 0)),
                  pl.BlockSpec((1, d), lambda i: (0, 0))],
        out_specs=pl.BlockSpec((tm, d), lambda i: (i, 0)),
        out_shape=jax.ShapeDtypeStruct((a.shape[0] + b.shape[0], d), out_dtype),
        compiler_params=_params(48), name="rmsnorm_stream")(a, b, w.reshape(1, d))


def _row_tile(rows, row0, cands):
    return _pick(math.gcd(row0, rows) if row0 else rows, cands)


def _add_rmsnorm(a, b, w, b_row0):
    rows, d = a.shape
    tm = _row_tile(rows, b_row0, (256, 128, 64, 8))
    blk0 = b_row0 // tm
    return pl.pallas_call(
        _add_rmsnorm_kernel, grid=(rows // tm,),
        in_specs=[pl.BlockSpec((tm, d), lambda i: (i, 0)), pl.BlockSpec((tm, d), lambda i: (blk0 + i, 0)),
                  pl.BlockSpec((1, d), lambda i: (0, 0))],
        out_specs=pl.BlockSpec((tm, d), lambda i: (i, 0)),
        out_shape=jax.ShapeDtypeStruct((rows, d), F32),
        compiler_params=_params(56), name="add_rmsnorm")(a, b, w.reshape(1, d))


def _add_kernel(a_ref, b_ref, o_ref):
    o_ref[...] = a_ref[...] + b_ref[...]


def _add(a, b, b_row0):
    rows, d = a.shape
    tm = _row_tile(rows, b_row0, (256, 128, 64, 8))
    blk0 = b_row0 // tm
    spec = pl.BlockSpec((tm, d), lambda i: (i, 0))
    return pl.pallas_call(_add_kernel, grid=(rows // tm,),
                          in_specs=[spec, pl.BlockSpec((tm, d), lambda i: (blk0 + i, 0))], out_specs=spec,
                          out_shape=jax.ShapeDtypeStruct((rows, d), F32),
                          compiler_params=_params(56), name="residual_add")(a, b)


def _mm_kernel(a_ref, b_ref, o_ref):
    o_ref[...] = jnp.dot(a_ref[...], b_ref[...].astype(a_ref.dtype), preferred_element_type=F32).astype(o_ref.dtype)


def _mm_res_kernel(a_ref, b_ref, r_ref, o_ref):
    o_ref[...] = r_ref[...] + jnp.dot(a_ref[...], b_ref[...], preferred_element_type=F32)


def _matmul(a, b, res=None, tm_cands=(1024, 832, 640, 512, 416, 320, 256, 128, 64, 8), tn=512, name="matmul"):
    m, k = a.shape
    n = b.shape[1]
    tm = _pick(m, tm_cands)
    tn = _pick(n, (tn, 256, 128))
    in_specs = [pl.BlockSpec((tm, k), lambda i, j: (i, 0)), pl.BlockSpec((k, tn), lambda i, j: (0, j))]
    args = [a, b]
    body = _mm_kernel
    if res is not None:
        in_specs.append(pl.BlockSpec((tm, tn), lambda i, j: (i, j)))
        args.append(res)
        body = _mm_res_kernel
    return pl.pallas_call(
        body, grid=(m // tm, n // tn), in_specs=in_specs,
        out_specs=pl.BlockSpec((tm, tn), lambda i, j: (i, j)),
        out_shape=jax.ShapeDtypeStruct((m, n), F32),
        compiler_params=_params(56), name=name)(*args)


def _ret_log_decay(h):
    return math.log(1.0 - 2.0 ** (-5.0 - h))


def _group_norm_gate(o, gnw, rg):
    mu = jnp.mean(o, axis=-1, keepdims=True)
    d = o - mu
    var = jnp.mean(d * d, axis=-1, keepdims=True)
    y = d * lax.rsqrt(var + NORM_EPS) * gnw
    return (y * (rg * jax.nn.sigmoid(rg))).astype(BF16)


def _ret_prompt_kernel(q_ref, k_ref, v_ref, rg_ref, gnw_ref, ar_ref, st_ref, state_scr):
    c = pl.program_id(1)
    L = RET_CHUNK

    @pl.when(c == 0)
    def _():
        state_scr[...] = jnp.zeros_like(state_scr)

    diff = (lax.broadcasted_iota(I32, (L, L), 0) - lax.broadcasted_iota(I32, (L, L), 1)).astype(F32)
    row_k = lax.broadcasted_iota(I32, (L, RET_DK), 0).astype(F32)
    row_v = lax.broadcasted_iota(I32, (L, RET_DV), 0).astype(F32)
    def first_matmuls(h):
        lg = _ret_log_decay(h)
        ck = slice(h * RET_DK, (h + 1) * RET_DK)
        q = q_ref[:, ck].astype(BF16)
        k = k_ref[:, ck] * (RET_DK ** -0.5)
        v = v_ref[:, h * RET_DV:(h + 1) * RET_DV].astype(BF16)
        s = lax.dot_general(q, k.astype(BF16), _NT, preferred_element_type=F32)
        st = state_scr[h]
        o_cross = jnp.dot(q, st.astype(BF16), preferred_element_type=F32)
        kd = (k * jnp.exp(lg * (L - 1.0 - row_k))).T.astype(BF16)
        state_scr[h] = st * math.exp(lg * L) + jnp.dot(kd, v, preferred_element_type=F32)
        return s, o_cross, v

    def decay_scores(h, s, o_cross, v):
        lg = _ret_log_decay(h)
        decay = jnp.where(diff >= 0, jnp.exp(lg * jnp.maximum(diff, 0.0)), 0.0)
        return (s * decay).astype(BF16), o_cross * jnp.exp(lg * (row_v + 1.0)), v

    def finish(h, inner, o_cross, v):
        cv = slice(h * RET_DV, (h + 1) * RET_DV)
        o = jnp.dot(inner, v, preferred_element_type=F32) + o_cross
        ar_ref[:, cv] = _group_norm_gate(o, gnw_ref[:, cv], rg_ref[:, cv])

    stage1, stage2 = {}, {}
    for step in range(RET_HEADS + 2):
        if step < RET_HEADS:
            stage1[step] = first_matmuls(step)
        if 0 <= step - 1 < RET_HEADS:
            stage2[step - 1] = decay_scores(step - 1, *stage1.pop(step - 1))
        if step - 2 >= 0:
            finish(step - 2, *stage2.pop(step - 2))

    @pl.when(c == pl.num_programs(1) - 1)
    def _():
        st_ref[0] = state_scr[...]


def _retention_prompt(proj, gnw, batch, seq):
    nc = seq // RET_CHUNK
    row = lambda n, c: n * nc + c
    return pl.pallas_call(
        _ret_prompt_kernel, grid=(batch, nc),
        in_specs=[pl.BlockSpec((RET_CHUNK, RET_QK), lambda n, c: (row(n, c), OFF_RQ // RET_QK)),
                  pl.BlockSpec((RET_CHUNK, RET_QK), lambda n, c: (row(n, c), OFF_RK // RET_QK)),
                  pl.BlockSpec((RET_CHUNK, RET_V), lambda n, c: (row(n, c), OFF_RV // RET_V)),
                  pl.BlockSpec((RET_CHUNK, RET_V), lambda n, c: (row(n, c), OFF_RG // RET_V)),
                  pl.BlockSpec((1, RET_V), lambda n, c: (0, 0))],
        out_specs=[pl.BlockSpec((RET_CHUNK, RET_V), lambda n, c: (row(n, c), 0)),
                   pl.BlockSpec((1, RET_HEADS, RET_DK, RET_DV), lambda n, c: (n, 0, 0, 0))],
        out_shape=[jax.ShapeDtypeStruct((batch * seq, RET_V), BF16),
                   jax.ShapeDtypeStruct((batch, RET_HEADS, RET_DK, RET_DV), F32)],
        scratch_shapes=[pltpu.VMEM((RET_HEADS, RET_DK, RET_DV), F32)],
        compiler_params=_params(32), name="retention_prompt")(proj, proj, proj, proj, gnw.reshape(1, RET_V))


def _ret_sample_kernel(q_ref, k_ref, v_ref, rg_ref, gnw_ref, st_in_ref, ar_ref, st_out_ref, o_scr, *, ts, ns_blk):
    i = pl.program_id(0)
    R = q_ref.shape[0]

    def sample_and_pos(shape, axis):
        r = lax.broadcasted_iota(I32, shape, axis).astype(F32)
        sid = jnp.floor((r + 0.5) * (1.0 / ts))
        return sid.astype(I32), r - sid * ts

    sid_r, t_r = sample_and_pos((R, R), 0)
    sid_c, t_c = sample_and_pos((R, R), 1)
    sid_k, t_k = sample_and_pos((R, RET_DK), 0)
    _, t_v = sample_and_pos((R, RET_DV), 0)
    for h in range(RET_HEADS):
        lg = _ret_log_decay(h)
        ck = slice(h * RET_DK, (h + 1) * RET_DK)
        cv = slice(h * RET_DV, (h + 1) * RET_DV)
        q = q_ref[:, ck]
        k = k_ref[:, ck] * (RET_DK ** -0.5)
        v = v_ref[:, cv].astype(BF16)

        @pl.when(i == 0)
        def _():
            s = lax.dot_general(q.astype(BF16), k.astype(BF16), _NT, preferred_element_type=F32)
            dt = t_r - t_c
            decay = jnp.where((sid_r == sid_c) & (dt >= 0), jnp.exp(lg * jnp.maximum(dt, 0.0)), 0.0)
            o_scr[:, cv] = jnp.dot((s * decay).astype(BF16), v, preferred_element_type=F32)

        q_dec = jnp.exp(lg * (t_v + 1.0))
        kd = k * jnp.exp(lg * (ts - 1.0 - t_k))
        for j in range(ns_blk):
            mine = sid_k == i * ns_blk + j
            st = st_in_ref[j, h]
            qm = jnp.where(mine, q, 0.0).astype(BF16)
            o_scr[:, cv] += jnp.dot(qm, st.astype(BF16), preferred_element_type=F32) * q_dec
            kdm = jnp.where(mine, kd, 0.0).T.astype(BF16)
            st_out_ref[j, h] = st * math.exp(lg * ts) + jnp.dot(kdm, v, preferred_element_type=F32)

    @pl.when(i == pl.num_programs(0) - 1)
    def _():
        for h in range(RET_HEADS):
            cv = slice(h * RET_DV, (h + 1) * RET_DV)
            ar_ref[:, cv] = _group_norm_gate(o_scr[:, cv], gnw_ref[:, cv], rg_ref[:, cv])


def _retention_sample(proj_s, gnw, state, ts):
    rows = proj_s.shape[0]
    ns = state.shape[0]
    ns_blk = _pick(ns, (4, 2, 1))
    return pl.pallas_call(
        functools.partial(_ret_sample_kernel, ts=ts, ns_blk=ns_blk), grid=(ns // ns_blk,),
        in_specs=[pl.BlockSpec((rows, RET_QK), lambda i: (0, OFF_RQ // RET_QK)),
                  pl.BlockSpec((rows, RET_QK), lambda i: (0, OFF_RK // RET_QK)),
                  pl.BlockSpec((rows, RET_V), lambda i: (0, OFF_RV // RET_V)),
                  pl.BlockSpec((rows, RET_V), lambda i: (0, OFF_RG // RET_V)),
                  pl.BlockSpec((1, RET_V), lambda i: (0, 0)),
                  pl.BlockSpec((ns_blk, RET_HEADS, RET_DK, RET_DV), lambda i: (i, 0, 0, 0))],
        out_specs=[pl.BlockSpec((rows, RET_V), lambda i: (0, 0)),
                   pl.BlockSpec((ns_blk, RET_HEADS, RET_DK, RET_DV), lambda i: (i, 0, 0, 0))],
        out_shape=[jax.ShapeDtypeStruct((rows, RET_V), BF16),
                   jax.ShapeDtypeStruct(state.shape, F32)],
        scratch_shapes=[pltpu.VMEM((rows, RET_V), F32)],
        compiler_params=_params(40), name="retention_sample")(
            proj_s, proj_s, proj_s, proj_s, gnw.reshape(1, RET_V), state)


def _alibi_slope(head):
    return 2.0 ** (-ALIBI_MAX_EXP * (head + 1.0) / DIL_HEADS)


def _dil_prompt_kernel(slope_ref, q_ref, k_ref, kp_ref, v_ref, vp_ref, o_ref, lse_ref, *, group, dil, win_steps, nq):
    j = pl.program_id(1)
    hh = pl.program_id(2)
    slope = slope_ref[group * DIL_HPG + hh]
    B = DIL_BLOCK
    qi = lax.broadcasted_iota(I32, (B, B), 0)
    kj = lax.broadcasted_iota(I32, (B, B), 1)
    steps_prev = qi + B - kj
    steps_cur = qi - kj
    in_prev = steps_prev <= win_steps
    in_prev_first = in_prev & (jnp.full((B, B), j, I32) > 0)
    in_cur = (steps_cur >= 0) & (steps_cur <= win_steps)
    bias_prev = slope * (steps_prev * dil).astype(F32)
    bias_cur = slope * (steps_cur * dil).astype(F32)
    scale = DIL_HD ** -0.5
    tiles = [(qb, r) for qb in range(nq) for r in range(dil)]

    def rows_of(qb, r):
        return pl.ds(qb * B * dil + r, B, stride=dil)

    def prev_of(ref, first_ref, qb, r):
        return first_ref[rows_of(0, r), :] if qb == 0 else ref[rows_of(qb - 1, r), :]

    def scores(qb, r):
        q = q_ref[rows_of(qb, r), :].astype(BF16)
        sp = lax.dot_general(q, prev_of(k_ref, kp_ref, qb, r).astype(BF16), _NT, preferred_element_type=F32)
        sc = lax.dot_general(q, k_ref[rows_of(qb, r), :].astype(BF16), _NT, preferred_element_type=F32)
        return sp, sc

    def softmax(qb, sp, sc):
        sp = jnp.where(in_prev_first if qb == 0 else in_prev, sp * scale - bias_prev, -jnp.inf)
        sc = jnp.where(in_cur, sc * scale - bias_cur, -jnp.inf)
        m = jnp.maximum(jnp.max(sp, axis=-1, keepdims=True), jnp.max(sc, axis=-1, keepdims=True))
        ep = jnp.exp(sp - m)
        ec = jnp.exp(sc - m)
        l = jnp.sum(ep, axis=-1, keepdims=True) + jnp.sum(ec, axis=-1, keepdims=True)
        return (ep / l).astype(BF16), (ec / l).astype(BF16), m + jnp.log(l)

    def finish(qb, r, pp, pc, lse):
        o = jnp.dot(pp, prev_of(v_ref, vp_ref, qb, r).astype(BF16), preferred_element_type=F32)
        o = o + jnp.dot(pc, v_ref[rows_of(qb, r), :].astype(BF16), preferred_element_type=F32)
        o_ref[rows_of(qb, r), :] = o
        lse_ref[rows_of(qb, r), :] = jnp.broadcast_to(lse, (B, DIL_HD))

    s_ready, p_ready = {}, {}
    lag_softmax, lag_finish = 2, 4
    for step in range(len(tiles) + lag_finish):
        if step < len(tiles):
            s_ready[step] = scores(*tiles[step])
        if 0 <= step - lag_softmax < len(tiles):
            p_ready[step - lag_softmax] = softmax(tiles[step - lag_softmax][0], *s_ready.pop(step - lag_softmax))
        if 0 <= step - lag_finish < len(tiles):
            finish(*tiles[step - lag_finish], *p_ready.pop(step - lag_finish))


def _dilated_prompt(proj, slopes, group, batch, seq):
    win, dil = DIL_GROUPS[group]
    span = DIL_BLOCK * dil
    assert seq % span == 0
    nq = _pick(seq // span, [n for n in (8, 4, 2, 1) if n * dil <= 16])
    nj = seq // (span * nq)
    col = lambda off: off // DIL_HD + group * DIL_HPG

    def cur(off):
        return pl.BlockSpec((span * nq, DIL_HD), lambda b, j, hh: (b * nj + j, col(off) + hh))

    def prev(off):
        return pl.BlockSpec((span, DIL_HD), lambda b, j, hh: (jnp.maximum((b * nj + j) * nq - 1, 0), col(off) + hh))

    out_spec = pl.BlockSpec((span * nq, DIL_HD), lambda b, j, hh: (b * nj + j, hh))
    return pl.pallas_call(
        functools.partial(_dil_prompt_kernel, group=group, dil=dil, win_steps=win // dil, nq=nq),
        grid=(batch, nj, DIL_HPG),
        in_specs=[pl.BlockSpec(memory_space=pltpu.SMEM), cur(OFF_DQ), cur(OFF_DK), prev(OFF_DK), cur(OFF_DV), prev(OFF_DV)],
        out_specs=[out_spec, out_spec],
        out_shape=[jax.ShapeDtypeStruct((batch * seq, DIL_W), F32)] * 2,
        compiler_params=_params(32), name=f"dilated_prompt_g{group}")(slopes, proj, proj, proj, proj, proj)


def _dil_sample_kernel(q_ref, kn_ref, vn_ref, kc_ref, vc_ref, o_ref, lse_ref, ok_ref, ov_ref,
                       *, group, dil, win_steps, ts):
    H = DIL_HPG
    buf = kc_ref.shape[0] // H
    P = q_ref.shape[0]
    t = lax.broadcasted_iota(I32, (P, buf), 0)
    r = lax.broadcasted_iota(I32, (P, buf), 1)
    diff = buf + t - r
    span = win_steps * dil
    ok = ((diff & (dil - 1)) == 0) & (diff <= span) & (diff >= 0)
    bias_steps = diff.astype(F32)
    t1 = lax.broadcasted_iota(I32, (P, 1), 0)
    scale = DIL_HD ** -0.5
    for hh in range(DIL_HPG):
        slope = _alibi_slope(group * DIL_HPG + hh)
        c = slice(hh * DIL_HD, (hh + 1) * DIL_HD)
        q = q_ref[:, c]
        head_rows = pl.ds(hh, buf, stride=H)
        s = lax.dot_general(q.astype(BF16), kc_ref[head_rows, :].astype(BF16), _NT, preferred_element_type=F32)
        s = jnp.where(ok, s * scale - slope * bias_steps, -jnp.inf)
        m = jnp.max(s, axis=-1, keepdims=True)
        s_new = []
        for rp in range(ts):
            d = t1 - rp
            ok_n = (d >= 0) & ((d & (dil - 1)) == 0) & (d <= span)
            new_row = slice(rp * H + hh, rp * H + hh + 1)
            sn = jnp.sum(q * kn_ref[new_row, :], axis=-1, keepdims=True) * scale - slope * d.astype(F32)
            sn = jnp.where(ok_n, sn, -jnp.inf)
            s_new.append(sn)
            m = jnp.maximum(m, sn)
        e = jnp.exp(s - m)
        e_new = [jnp.exp(sn - m) for sn in s_new]
        l = jnp.sum(e, axis=-1, keepdims=True)
        for en in e_new:
            l = l + en
        o = jnp.dot((e / l).astype(BF16), vc_ref[head_rows, :].astype(BF16), preferred_element_type=F32)
        for rp in range(ts):
            o = o + (e_new[rp] / l) * vn_ref[rp * H + hh:rp * H + hh + 1, :]
        o_ref[:, c] = o
        lse_ref[:, c] = jnp.broadcast_to(m + jnp.log(l), (P, DIL_HD))
    ok_ref[0:(buf - ts) * H, :] = kc_ref[ts * H:buf * H, :]
    ok_ref[(buf - ts) * H:buf * H, :] = kn_ref[...]
    ov_ref[0:(buf - ts) * H, :] = vc_ref[ts * H:buf * H, :]
    ov_ref[(buf - ts) * H:buf * H, :] = vn_ref[...]


def _dilated_sample(q8, kn, vn, cache_k, cache_v, layer, group, ts):
    win, dil = DIL_GROUPS[group]
    depth, ns, buf = cache_k.shape[:3]
    assert buf == win and buf % dil == 0, "cached window must hold exactly one full window"
    flat = lambda c: c.reshape(depth, ns, buf * DIL_HPG, DIL_HD)
    small = pl.BlockSpec((None, SAMPLE_PAD, DIL_W), lambda n: (n, 0, 0))
    new = pl.BlockSpec((None, ts * DIL_HPG, DIL_HD), lambda n: (n, 0, 0))
    big_in = pl.BlockSpec((None, None, buf * DIL_HPG, DIL_HD), lambda n: (layer, n, 0, 0))
    big_out = pl.BlockSpec((None, buf * DIL_HPG, DIL_HD), lambda n: (n, 0, 0))
    o, lse, k_out, v_out = pl.pallas_call(
        functools.partial(_dil_sample_kernel, group=group, dil=dil, win_steps=win // dil, ts=ts),
        grid=(ns,), in_specs=[small, new, new, big_in, big_in],
        out_specs=[small, small, big_out, big_out],
        out_shape=[jax.ShapeDtypeStruct((ns, SAMPLE_PAD, DIL_W), F32)] * 2
        + [jax.ShapeDtypeStruct((ns, buf * DIL_HPG, DIL_HD), F32)] * 2,
        compiler_params=_params(56), name=f"dilated_sample_g{group}")(q8, kn, vn, flat(cache_k), flat(cache_v))
    return o, lse, k_out.reshape(ns, buf, DIL_HPG, DIL_HD), v_out.reshape(ns, buf, DIL_HPG, DIL_HD)


def _combine_kernel(o1_ref, o2_ref, o3_ref, l1_ref, l2_ref, l3_ref, out_ref):
    l1, l2, l3 = l1_ref[...], l2_ref[...], l3_ref[...]
    m = jnp.maximum(jnp.maximum(l1, l2), l3)
    e1, e2, e3 = jnp.exp(l1 - m), jnp.exp(l2 - m), jnp.exp(l3 - m)
    z = e1 + e2 + e3
    out = (e1 / z) * o1_ref[...] + (e2 / z) * o2_ref[...] + (e3 / z) * o3_ref[...]
    out_ref[...] = out.astype(out_ref.dtype)


def _combine_groups(outs, lses):
    m, w = outs[0].shape
    tm = _pick(m, (1024, 512, 256, 128, 64, 8))
    spec = pl.BlockSpec((tm, w), lambda i: (i, 0))
    return pl.pallas_call(_combine_kernel, grid=(m // tm,), in_specs=[spec] * 6, out_specs=spec,
                          out_shape=jax.ShapeDtypeStruct((m, w), BF16),
                          compiler_params=_params(40), name="combine_groups")(*outs, *lses)


def _mem_attn_head(q, k, v):
    s = lax.dot_general(q.astype(BF16), k.astype(BF16), _NT, preferred_element_type=F32) * (MEM_HD ** -0.5)
    e = jnp.exp(s - jnp.max(s, axis=-1, keepdims=True))
    p = e / jnp.sum(e, axis=-1, keepdims=True)
    return jnp.dot(p.astype(BF16), v.astype(BF16), preferred_element_type=F32)


def _mem_attn_kernel(q_ref, k_ref, v_ref, o_ref):
    for h in range(MEM_HEADS):
        c = slice(h * MEM_HD, (h + 1) * MEM_HD)
        o_ref[:, c] = _mem_attn_head(q_ref[:, c], k_ref[:, c], v_ref[:, c]).astype(o_ref.dtype)


def _mem_attn_cached_kernel(q_ref, k_ref, v_ref, o_ref):
    chunks = MEM_HD // 128
    mem_len = k_ref.shape[0] // (MEM_HEADS * chunks)
    for h in range(MEM_HEADS):
        rows = [pl.ds(h * chunks + c, mem_len, stride=MEM_HEADS * chunks) for c in range(chunks)]
        cols = [slice(h * MEM_HD + c * 128, h * MEM_HD + (c + 1) * 128) for c in range(chunks)]
        s = sum(lax.dot_general(q_ref[:, cols[c]].astype(BF16), k_ref[rows[c], :].astype(BF16), _NT,
                                preferred_element_type=F32) for c in range(chunks)) * (MEM_HD ** -0.5)
        e = jnp.exp(s - jnp.max(s, axis=-1, keepdims=True))
        p = (e / jnp.sum(e, axis=-1, keepdims=True)).astype(BF16)
        for c in range(chunks):
            o_ref[:, cols[c]] = jnp.dot(p, v_ref[rows[c], :].astype(BF16),
                                        preferred_element_type=F32).astype(o_ref.dtype)


def _memory_attend_prompt(proj, mkv, batch, seq):
    mem_len = mkv.shape[1]
    tq = _pick(seq, (512, 256, 128))
    nq = seq // tq
    return pl.pallas_call(
        _mem_attn_kernel, grid=(batch, nq),
        in_specs=[pl.BlockSpec((tq, MEM_W), lambda b, i: (b * nq + i, OFF_MQ // MEM_W)),
                  pl.BlockSpec((None, mem_len, MEM_W), lambda b, i: (b, 0, 0)),
                  pl.BlockSpec((None, mem_len, MEM_W), lambda b, i: (b, 0, 1))],
        out_specs=pl.BlockSpec((tq, MEM_W), lambda b, i: (b * nq + i, 0)),
        out_shape=jax.ShapeDtypeStruct((batch * seq, MEM_W), BF16),
        compiler_params=_params(40), name="memory_attend_prompt")(proj, mkv, mkv)


def _memory_attend_sample(q8, mem_k, mem_v, layer):
    depth, ns, mem_len = mem_k.shape[:3]
    rows = mem_len * MEM_W // 128
    flat = lambda c: c.reshape(depth, ns, rows, 128)
    small = pl.BlockSpec((None, SAMPLE_PAD, MEM_W), lambda n: (n, 0, 0))
    big = pl.BlockSpec((None, None, rows, 128), lambda n: (layer, n, 0, 0))
    return pl.pallas_call(
        _mem_attn_cached_kernel, grid=(ns,), in_specs=[small, big, big], out_specs=small,
        out_shape=jax.ShapeDtypeStruct((ns, SAMPLE_PAD, MEM_W), F32),
        compiler_params=_params(24), name="memory_attend_sample")(q8, flat(mem_k), flat(mem_v))


def _mix_kernel(ar_ref, ad_ref, am_ref, wr_ref, wd_ref, wm_ref, gr_ref, gd_ref, gm_ref, o_ref):
    r = jnp.dot(ar_ref[...], wr_ref[...], preferred_element_type=F32)
    d = jnp.dot(ad_ref[...], wd_ref[...], preferred_element_type=F32)
    m = jnp.dot(am_ref[...], wm_ref[...], preferred_element_type=F32)
    mix = jax.nn.sigmoid(gr_ref[...]) * r + jax.nn.sigmoid(gd_ref[...]) * d + jax.nn.sigmoid(gm_ref[...]) * m
    o_ref[...] = mix.astype(o_ref.dtype)


def _branch_mix(a_r, a_d, a_m, w_r, w_d, w_m, proj, row0):
    t = a_r.shape[0]
    tm = _row_tile(t, row0, (1024, 832, 640, 512, 416, 320, 256, 128, 64, 8))
    blk0 = row0 // tm
    tn = 512
    nj = D_MODEL // tn
    lhs = lambda w: pl.BlockSpec((tm, w), lambda i, j: (i, 0))
    rhs = lambda w: pl.BlockSpec((w, tn), lambda i, j: (0, j))
    gate = lambda b: pl.BlockSpec((tm, tn), lambda i, j: (blk0 + i, OFF_GATE // tn + b * nj + j))
    return pl.pallas_call(
        _mix_kernel, grid=(t // tm, nj),
        in_specs=[lhs(RET_V), lhs(DIL_W), lhs(MEM_W), rhs(RET_V), rhs(DIL_W), rhs(MEM_W), gate(0), gate(1), gate(2)],
        out_specs=pl.BlockSpec((tm, tn), lambda i, j: (i, j)),
        out_shape=jax.ShapeDtypeStruct((t, D_MODEL), BF16),
        compiler_params=_params(56), name="branch_mix")(a_r, a_d, a_m, w_r, w_d, w_m, proj, proj, proj)


def _staircase(k):
    return [(a, k // (a + 1)) for a in range(k)]


def _peer_topk_kernel(q_ref, keys_ref, i_ref, j_ref, g_ref, s_scr, key_scr, cand_scr, pos_scr):
    tk = q_ref.shape[0]
    K = PEER_TOPK
    SUB = 8
    iota_k = lax.broadcasted_iota(I32, (K, tk), 0)

    def top_k_rows(score_refs, id_ref):
        def body(k, carry):
            sel = iota_k == k
            out = []
            for ref, (prev, top_s, top_i) in zip(score_refs, carry):
                best = best_id = None
                for v in range(ref.shape[0] // SUB):
                    rows = slice(v * SUB, (v + 1) * SUB)
                    ids = id_ref[rows, :]
                    sv = jnp.where(ids == prev, -jnp.inf, ref[rows, :])
                    ref[rows, :] = sv
                    if best is None:
                        best, best_id = sv, ids
                    else:
                        best_id = jnp.where(sv > best, ids, best_id)
                        best = jnp.maximum(best, sv)
                m = jnp.max(best, axis=0, keepdims=True)
                pos = jnp.min(jnp.where(best == m, best_id, jnp.iinfo(jnp.int32).max), axis=0, keepdims=True)
                out.append((pos, jnp.where(sel, m, top_s), jnp.where(sel, pos, top_i)))
            return tuple(out)

        init = tuple((jnp.full((1, tk), -1, I32), jnp.zeros((K, tk), F32), jnp.zeros((K, tk), I32))
                     for _ in score_refs)
        return [(top_s, top_i) for _, top_s, top_i in lax.fori_loop(0, K, body, init)]

    for c in range(2):
        qc = q_ref[:, c * PEER_NKEYS:(c + 1) * PEER_NKEYS].astype(BF16)
        s_scr[c] = lax.dot_general(keys_ref[0, c].astype(BF16), qc, _NT, preferred_element_type=F32)
    key_scr[...] = lax.broadcasted_iota(I32, key_scr.shape, 0)
    (s1, i1), (s2, i2) = top_k_rows([s_scr.at[0], s_scr.at[1]], key_scr)

    cand_scr[...] = jnp.full(cand_scr.shape, -jnp.inf, F32)
    pos_scr[...] = jnp.full(pos_scr.shape, K * K, I32)
    off = 0
    for a, nb in _staircase(K):
        cand_scr[off:off + nb, :] = s1[a:a + 1, :] + s2[0:nb, :]
        pos_scr[off:off + nb, :] = lax.broadcasted_iota(I32, (nb, tk), 0) + a * K
        off += nb
    (best_s, best_p), = top_k_rows([cand_scr], pos_scr)
    pa = jnp.right_shift(best_p, K.bit_length() - 1)
    pb = best_p - pa * K
    ei = jnp.zeros((K, tk), I32)
    ej = jnp.zeros((K, tk), I32)
    for a in range(K):
        ei = jnp.where(pa == a, i1[a:a + 1, :], ei)
        ej = jnp.where(pb == a, i2[a:a + 1, :], ej)
    e = jnp.exp(best_s - jnp.max(best_s, axis=0, keepdims=True))
    i_ref[...] = ei.astype(F32)
    j_ref[...] = ej.astype(F32)
    g_ref[...] = e / jnp.sum(e, axis=0, keepdims=True)


def _peer_route(q, sub_keys):
    t = q.shape[0]
    tk = _pick(t, (640, 512, 256, 128))
    n_cand = -(-sum(nb for _, nb in _staircase(PEER_TOPK)) // 8) * 8
    out_spec = pl.BlockSpec((PEER_TOPK, tk), lambda i, h: (h, i))
    return pl.pallas_call(
        _peer_topk_kernel, grid=(t // tk, PEER_HEADS),
        in_specs=[pl.BlockSpec((tk, PEER_DQ), lambda i, h: (i, h)),
                  pl.BlockSpec((1, 2, PEER_NKEYS, PEER_DQ // 2), lambda i, h: (h, 0, 0, 0))],
        out_specs=[out_spec] * 3,
        out_shape=[jax.ShapeDtypeStruct((PEER_HEADS * PEER_TOPK, t), F32)] * 3,
        scratch_shapes=[pltpu.VMEM((2, PEER_NKEYS, tk), F32), pltpu.VMEM((PEER_NKEYS, tk), I32),
                        pltpu.VMEM((n_cand, tk), F32), pltpu.VMEM((n_cand, tk), I32)],
        compiler_params=_params(24), name="peer_route")(q, sub_keys)


def _peer_wbuild_kernel(i_ref, j_ref, g_ref, w_ref, it_scr, jt_scr, gt_scr, wt_scr):
    tw = w_ref.shape[0]
    it_scr[...] = i_ref[...].T
    jt_scr[...] = j_ref[...].T
    gt_scr[...] = g_ref[...].T
    n = PEER_NKEYS
    iota_r = lax.broadcasted_iota(I32, (n, i_ref.shape[0]), 0).astype(F32)

    def body(t, carry):
        irow = it_scr[pl.ds(t, 1), :]
        jrow = jt_scr[pl.ds(t, 1), :]
        grow = gt_scr[pl.ds(t, 1), :]
        a_t = jnp.where(iota_r == irow, grow, 0.0).astype(BF16)
        b_t = jnp.where(iota_r == jrow, 1.0, 0.0).astype(BF16)
        w = lax.dot_general(a_t, b_t, _NT, preferred_element_type=F32)
        wt_scr[pl.ds(pl.multiple_of(t * W_ROW_PITCH, 8), n), :] = w
        return carry

    lax.fori_loop(0, tw, body, 0, unroll=8)
    for i in range(n):
        w_ref[:, i * n:(i + 1) * n] = wt_scr[pl.ds(i, tw, stride=W_ROW_PITCH), :].astype(BF16)


def _peer_wbuild(ei, ej, g):
    nsel, t = ei.shape
    tw = 128
    in_spec = pl.BlockSpec((nsel, tw), lambda i: (0, i))
    n = PEER_NKEYS
    return pl.pallas_call(
        _peer_wbuild_kernel, grid=(t // tw,), in_specs=[in_spec] * 3,
        out_specs=pl.BlockSpec((tw, n * n), lambda i: (i, 0)),
        out_shape=jax.ShapeDtypeStruct((t, n * n), BF16),
        scratch_shapes=[pltpu.VMEM((tw, nsel), F32)] * 3 + [pltpu.VMEM((tw * W_ROW_PITCH, n), F32)],
        compiler_params=_params(32), name="peer_wbuild")(ei, ej, g)


def _gelu_tanh(x):
    return x * (0.5 * (1.0 + jnp.tanh(math.sqrt(2.0 / math.pi) * (x + 0.044715 * (x * x * x)))))


def _peer_mlp_kernel(x_ref, u_ref, v_ref, w_ref, o_ref):
    e = pl.program_id(1)

    @pl.when(e == 0)
    def _():
        o_ref[...] = jnp.zeros_like(o_ref)

    half = u_ref.shape[0] // 2
    ps = []
    for c in range(2):
        rows = slice(c * half, (c + 1) * half)
        act = lax.dot_general(x_ref[...], u_ref[rows, :], _NT, preferred_element_type=F32)
        ps.append((w_ref[:, rows].astype(F32) * _gelu_tanh(act)).astype(BF16))
    p = jnp.concatenate(ps, axis=1)
    o_ref[...] = o_ref[...] + jnp.dot(p, v_ref[...], preferred_element_type=F32)


def _peer_mlp(n2, u, v, w):
    t, d = n2.shape
    ne = u.shape[0]
    tm = _pick(t, (640, 416, 320, 256, 128, 64, 8))
    eb = 512
    return pl.pallas_call(
        _peer_mlp_kernel, grid=(t // tm, ne // eb),
        in_specs=[pl.BlockSpec((tm, d), lambda i, e: (i, 0)),
                  pl.BlockSpec((eb, d), lambda i, e: (e, 0)),
                  pl.BlockSpec((eb, d), lambda i, e: (e, 0)),
                  pl.BlockSpec((tm, eb), lambda i, e: (i, e))],
        out_specs=pl.BlockSpec((tm, d), lambda i, e: (i, 0)),
        out_shape=jax.ShapeDtypeStruct((t, d), F32),
        compiler_params=_params(62), name="peer_mlp")(n2, u, v, w)


def _pad_sample_rows(a, ns, ts):
    a = a.reshape(ns, ts, a.shape[-1])
    return jnp.pad(a, ((0, 0), (0, SAMPLE_PAD - ts), (0, 0)))


def kernel(x_prompt, x_sample, mem_prompt, cache_ret_state, cache_win_k1, cache_win_v1, cache_win_k2, cache_win_v2, cache_win_k3, cache_win_v3, cache_mem_k, cache_mem_v, norm1_w, w_in, ret_gn_w, w_ret_o, w_dil_o, mem_norm_w, w_mem_kv, w_mem_o, w_out, norm2_w, w_peer_q, peer_sub_keys, peer_u, peer_v, final_norm_w):
    batch, seq, d = x_prompt.shape
    ns, ts, _ = x_sample.shape
    mem_len = mem_prompt.shape[1]
    depth = norm1_w.shape[0]
    assert d == D_MODEL and seq % RET_CHUNK == 0 and ts <= SAMPLE_PAD
    tp, tsamp = batch * seq, ns * ts
    win_k = (cache_win_k1, cache_win_k2, cache_win_k3)
    win_v = (cache_win_v1, cache_win_v2, cache_win_v3)

    slopes = jnp.asarray([_alibi_slope(h) for h in range(DIL_HEADS)], F32)
    x_p, x_s = x_prompt.reshape(tp, d), x_sample.reshape(tsamp, d)
    prompt_states, sample_states = [], []
    for l in range(depth):
        proj = _matmul(_rmsnorm_pair(x_p, x_s, norm1_w[l], BF16), w_in[l],
                       tm_cands=(1664, 832, 640, 512, 416, 320, 256, 128, 64, 8), name="input_projection")
        proj_s = proj[tp:]

        ar_p, ret_p = _retention_prompt(proj, ret_gn_w[l], batch, seq)
        ar_s, ret_s = _retention_sample(proj_s, ret_gn_w[l], cache_ret_state[l], ts)

        outs_p, lses_p, outs_s, lses_s, bufs_p, bufs_s = [], [], [], [], [], []
        for g, (win, dil) in enumerate(DIL_GROUPS):
            o, lse = _dilated_prompt(proj, slopes, g, batch, seq)
            outs_p.append(o)
            lses_p.append(lse)
            keep = min(win, seq)
            for off in (OFF_DK, OFF_DV):
                c0 = off + g * DIL_W
                tail = [proj[b * seq + seq - keep:(b + 1) * seq, c0:c0 + DIL_W] for b in range(batch)]
                bufs_p.append(jnp.stack(tail, axis=0).reshape(batch, keep, DIL_HPG, DIL_HD))
            q_s, k_s, v_s = (proj_s[:, off + g * DIL_W: off + (g + 1) * DIL_W] for off in (OFF_DQ, OFF_DK, OFF_DV))
            o8, lse8, k_new, v_new = _dilated_sample(
                _pad_sample_rows(q_s, ns, ts), k_s.reshape(ns, ts * DIL_HPG, DIL_HD), v_s.reshape(ns, ts * DIL_HPG, DIL_HD),
                win_k[g], win_v[g], l, g, ts)
            outs_s.append(o8[:, :ts].reshape(tsamp, DIL_W))
            lses_s.append(lse8[:, :ts].reshape(tsamp, DIL_W))
            bufs_s += [k_new, v_new]
        ad_p = _combine_groups(outs_p, lses_p)
        ad_s = _combine_groups(outs_s, lses_s)

        mem_n = _rmsnorm(mem_prompt.reshape(batch * mem_len, d), mem_norm_w[l], BF16)
        mkv = _matmul(mem_n, w_mem_kv[l].astype(BF16), name="memory_kv").reshape(batch, mem_len, 2 * MEM_W)
        am_p = _memory_attend_prompt(proj, mkv, batch, seq)
        mq8 = _pad_sample_rows(proj_s[:, OFF_MQ:OFF_MQ + MEM_W], ns, ts)
        am_s = _memory_attend_sample(mq8, cache_mem_k, cache_mem_v, l)
        am_s = am_s[:, :ts].reshape(tsamp, MEM_W).astype(BF16)

        branch_w = (w_ret_o[l].astype(BF16), w_dil_o[l].astype(BF16), w_mem_o[l].astype(BF16))
        w_out_b = w_out[l].astype(BF16)
        h_p = _matmul(_branch_mix(ar_p, ad_p, am_p, *branch_w, proj, 0), w_out_b, res=x_p, name="output_projection")
        h_s = _matmul(_branch_mix(ar_s, ad_s, am_s, *branch_w, proj, tp), w_out_b, res=x_s, name="output_projection")

        n2 = _rmsnorm_pair(h_p, h_s, norm2_w[l], BF16)
        pq = _matmul(n2, w_peer_q[l].astype(BF16), name="peer_query")
        ei, ej, gw = _peer_route(pq, peer_sub_keys[l])
        w_dense = _peer_wbuild(ei, ej, gw)
        peer = _peer_mlp(n2, peer_u[l].astype(BF16), peer_v[l].astype(BF16), w_dense)

        mk = mkv[:, :, :MEM_W].reshape(batch, mem_len, MEM_HEADS, MEM_HD)
        mv = mkv[:, :, MEM_W:].reshape(batch, mem_len, MEM_HEADS, MEM_HD)
        prompt_states.append((ret_p, *bufs_p, mk, mv))
        sample_states.append((ret_s, *bufs_s))
        if l + 1 < depth:
            x_p, x_s = _add(h_p, peer, 0), _add(h_s, peer, tp)

    y_prompt = _add_rmsnorm(h_p, peer, final_norm_w, 0).reshape(batch, seq, d)
    y_sample = _add_rmsnorm(h_s, peer, final_norm_w, tp).reshape(ns, ts, d)
    p_out = [jnp.stack(s, axis=0) for s in zip(*prompt_states)]
    s_out = [jnp.stack(s, axis=0) for s in zip(*sample_states)]
    return (y_prompt, y_sample, *p_out, *s_out)
```

```python
import functools
import math

import jax
import jax.numpy as jnp
from jax import lax
from jax.experimental import pallas as pl
from jax.experimental.pallas import tpu as pltpu

F32, BF16, I32 = jnp.float32, jnp.bfloat16, jnp.int32

D_MODEL = 4096
RET_HEADS, RET_DK, RET_DV, RET_CHUNK = 8, 128, 256, 128
DIL_GROUPS = ((128, 1), (512, 4), (2048, 16))
DIL_HPG, DIL_HD, DIL_BLOCK = 4, 128, 128
DIL_HEADS = DIL_HPG * len(DIL_GROUPS)
ALIBI_MAX_EXP = 8.0
MEM_HEADS, MEM_HD = 4, 384
PEER_HEADS, PEER_NKEYS, PEER_TOPK, PEER_DQ = 8, 128, 16, 256
NORM_EPS = 1e-6

RET_QK = RET_HEADS * RET_DK
RET_V = RET_HEADS * RET_DV
DIL_W = DIL_HPG * DIL_HD
DIL_ALL = DIL_HEADS * DIL_HD
MEM_W = MEM_HEADS * MEM_HD
OFF_RQ = 0
OFF_RK = OFF_RQ + RET_QK
OFF_RV = OFF_RK + RET_QK
OFF_RG = OFF_RV + RET_V
OFF_DQ = OFF_RG + RET_V
OFF_DK = OFF_DQ + DIL_ALL
OFF_DV = OFF_DK + DIL_ALL
OFF_MQ = OFF_DV + DIL_ALL
OFF_GATE = OFF_MQ + MEM_W
PROJ_W = OFF_GATE + 3 * D_MODEL

SAMPLE_PAD = 8
W_ROW_PITCH = PEER_NKEYS + 8
V7X_VMEM_BYTES = 64 * 1024 * 1024
MIB = 1024 * 1024

_NT = (((1,), (1,)), ((), ()))


def _pick(n, cands):
    for c in cands:
        if n % c == 0:
            return c
    raise ValueError(f"no tile in {cands} divides {n}")


def _params(vmem_mib):
    return pltpu.CompilerParams(vmem_limit_bytes=min(vmem_mib * MIB, V7X_VMEM_BYTES - 2 * MIB))


def _rmsnorm_kernel(x_ref, w_ref, o_ref):
    x = x_ref[...]
    ms = jnp.mean(x * x, axis=-1, keepdims=True)
    o_ref[...] = (x * lax.rsqrt(ms + NORM_EPS) * w_ref[...]).astype(o_ref.dtype)


def _rmsnorm(x, w, out_dtype):
    m, d = x.shape
    tm = _pick(m, (320, 256, 128, 64, 8))
    return pl.pallas_call(
        _rmsnorm_kernel, grid=(m // tm,),
        in_specs=[pl.BlockSpec((tm, d), lambda i: (i, 0)), pl.BlockSpec((1, d), lambda i: (0, 0))],
        out_specs=pl.BlockSpec((tm, d), lambda i: (i, 0)),
        out_shape=jax.ShapeDtypeStruct((m, d), out_dtype),
        compiler_params=_params(48), name="rmsnorm")(x, w.reshape(1, d))


def _add_rmsnorm_kernel(a_ref, b_ref, w_ref, o_ref):
    x = a_ref[...] + b_ref[...]
    ms = jnp.mean(x * x, axis=-1, keepdims=True)
    o_ref[...] = x * lax.rsqrt(ms + NORM_EPS) * w_ref[...]


def _rmsnorm_pair_kernel(a_ref, b_ref, w_ref, o_ref, *, na):
    i = pl.program_id(0)

    def norm(x):
        ms = jnp.mean(x * x, axis=-1, keepdims=True)
        return (x * lax.rsqrt(ms + NORM_EPS) * w_ref[...]).astype(o_ref.dtype)

    @pl.when(i < na)
    def _():
        o_ref[...] = norm(a_ref[...])

    @pl.when(i >= na)
    def _():
        o_ref[...] = norm(b_ref[...])


def _rmsnorm_pair(a, b, w, out_dtype):
    d = a.shape[1]
    tm = _pick(math.gcd(a.shape[0], b.shape[0]), (256, 128, 64, 8))
    na, nb = a.shape[0] // tm, b.shape[0] // tm
    return pl.pallas_call(
        functools.partial(_rmsnorm_pair_kernel, na=na), grid=(na + nb,),
        in_specs=[pl.BlockSpec((tm, d), lambda i: (jnp.minimum(i, na - 1), 0)),
                  pl.BlockSpec((tm, d), lambda i: (jnp.maximum(i - na, 0), 0)),
                  pl.BlockSpec((1, d), lambda i: (0, 0))],
        out_specs=pl.BlockSpec((tm, d), lambda i: (i, 0)),
        out_shape=jax.ShapeDtypeStruct((a.shape[0] + b.shape[0], d), out_dtype),
        compiler_params=_params(48), name="rmsnorm_stream")(a, b, w.reshape(1, d))


def _row_tile(rows, row0, cands):
    return _pick(math.gcd(row0, rows) if row0 else rows, cands)


def _add_rmsnorm(a, b, w, b_row0):
    rows, d = a.shape
    tm = _row_tile(rows, b_row0, (256, 128, 64, 8))
    blk0 = b_row0 // tm
    return pl.pallas_call(
        _add_rmsnorm_kernel, grid=(rows // tm,),
        in_specs=[pl.BlockSpec((tm, d), lambda i: (i, 0)), pl.BlockSpec((tm, d), lambda i: (blk0 + i, 0)),
                  pl.BlockSpec((1, d), lambda i: (0, 0))],
        out_specs=pl.BlockSpec((tm, d), lambda i: (i, 0)),
        out_shape=jax.ShapeDtypeStruct((rows, d), F32),
        compiler_params=_params(56), name="add_rmsnorm")(a, b, w.reshape(1, d))


def _add_kernel(a_ref, b_ref, o_ref):
    o_ref[...] = a_ref[...] + b_ref[...]


def _add(a, b, b_row0):
    rows, d = a.shape
    tm = _row_tile(rows, b_row0, (256, 128, 64, 8))
    blk0 = b_row0 // tm
    spec = pl.BlockSpec((tm, d), lambda i: (i, 0))
    return pl.pallas_call(_add_kernel, grid=(rows // tm,),
                          in_specs=[spec, pl.BlockSpec((tm, d), lambda i: (blk0 + i, 0))], out_specs=spec,
                          out_shape=jax.ShapeDtypeStruct((rows, d), F32),
                          compiler_params=_params(56), name="residual_add")(a, b)


def _mm_kernel(a_ref, b_ref, o_ref):
    o_ref[...] = jnp.dot(a_ref[...], b_ref[...].astype(a_ref.dtype), preferred_element_type=F32).astype(o_ref.dtype)


def _mm_res_kernel(a_ref, b_ref, r_ref, o_ref):
    o_ref[...] = r_ref[...] + jnp.dot(a_ref[...], b_ref[...], preferred_element_type=F32)


def _matmul(a, b, res=None, tm_cands=(1024, 832, 640, 512, 416, 320, 256, 128, 64, 8), tn=512, name="matmul"):
    m, k = a.shape
    n = b.shape[1]
    tm = _pick(m, tm_cands)
    tn = _pick(n, (tn, 256, 128))
    in_specs = [pl.BlockSpec((tm, k), lambda i, j: (i, 0)), pl.BlockSpec((k, tn), lambda i, j: (0, j))]
    args = [a, b]
    body = _mm_kernel
    if res is not None:
        in_specs.append(pl.BlockSpec((tm, tn), lambda i, j: (i, j)))
        args.append(res)
        body = _mm_res_kernel
    return pl.pallas_call(
        body, grid=(m // tm, n // tn), in_specs=in_specs,
        out_specs=pl.BlockSpec((tm, tn), lambda i, j: (i, j)),
        out_shape=jax.ShapeDtypeStruct((m, n), F32),
        compiler_params=_params(56), name=name)(*args)


def _ret_log_decay(h):
    return math.log(1.0 - 2.0 ** (-5.0 - h))


def _group_norm_gate(o, gnw, rg):
    mu = jnp.mean(o, axis=-1, keepdims=True)
    d = o - mu
    var = jnp.mean(d * d, axis=-1, keepdims=True)
    y = d * lax.rsqrt(var + NORM_EPS) * gnw
    return (y * (rg * jax.nn.sigmoid(rg))).astype(BF16)


def _ret_prompt_kernel(q_ref, k_ref, v_ref, rg_ref, gnw_ref, ar_ref, st_ref, state_scr):
    c = pl.program_id(1)
    L = RET_CHUNK

    @pl.when(c == 0)
    def _():
        state_scr[...] = jnp.zeros_like(state_scr)

    diff = (lax.broadcasted_iota(I32, (L, L), 0) - lax.broadcasted_iota(I32, (L, L), 1)).astype(F32)
    row_k = lax.broadcasted_iota(I32, (L, RET_DK), 0).astype(F32)
    row_v = lax.broadcasted_iota(I32, (L, RET_DV), 0).astype(F32)
    def first_matmuls(h):
        lg = _ret_log_decay(h)
        ck = slice(h * RET_DK, (h + 1) * RET_DK)
        q = q_ref[:, ck].astype(BF16)
        k = k_ref[:, ck] * (RET_DK ** -0.5)
        v = v_ref[:, h * RET_DV:(h + 1) * RET_DV].astype(BF16)
        s = lax.dot_general(q, k.astype(BF16), _NT, preferred_element_type=F32)
        st = state_scr[h]
        o_cross = jnp.dot(q, st.astype(BF16), preferred_element_type=F32)
        kd = (k * jnp.exp(lg * (L - 1.0 - row_k))).T.astype(BF16)
        state_scr[h] = st * math.exp(lg * L) + jnp.dot(kd, v, preferred_element_type=F32)
        return s, o_cross, v

    def decay_scores(h, s, o_cross, v):
        lg = _ret_log_decay(h)
        decay = jnp.where(diff >= 0, jnp.exp(lg * jnp.maximum(diff, 0.0)), 0.0)
        return (s * decay).astype(BF16), o_cross * jnp.exp(lg * (row_v + 1.0)), v

    def finish(h, inner, o_cross, v):
        cv = slice(h * RET_DV, (h + 1) * RET_DV)
        o = jnp.dot(inner, v, preferred_element_type=F32) + o_cross
        ar_ref[:, cv] = _group_norm_gate(o, gnw_ref[:, cv], rg_ref[:, cv])

    stage1, stage2 = {}, {}
    for step in range(RET_HEADS + 2):
        if step < RET_HEADS:
            stage1[step] = first_matmuls(step)
        if 0 <= step - 1 < RET_HEADS:
            stage2[step - 1] = decay_scores(step - 1, *stage1.pop(step - 1))
        if step - 2 >= 0:
            finish(step - 2, *stage2.pop(step - 2))

    @pl.when(c == pl.num_programs(1) - 1)
    def _():
        st_ref[0] = state_scr[...]


def _retention_prompt(proj, gnw, batch, seq):
    nc = seq // RET_CHUNK
    row = lambda n, c: n * nc + c
    return pl.pallas_call(
        _ret_prompt_kernel, grid=(batch, nc),
        in_specs=[pl.BlockSpec((RET_CHUNK, RET_QK), lambda n, c: (row(n, c), OFF_RQ // RET_QK)),
                  pl.BlockSpec((RET_CHUNK, RET_QK), lambda n, c: (row(n, c), OFF_RK // RET_QK)),
                  pl.BlockSpec((RET_CHUNK, RET_V), lambda n, c: (row(n, c), OFF_RV // RET_V)),
                  pl.BlockSpec((RET_CHUNK, RET_V), lambda n, c: (row(n, c), OFF_RG // RET_V)),
                  pl.BlockSpec((1, RET_V), lambda n, c: (0, 0))],
        out_specs=[pl.BlockSpec((RET_CHUNK, RET_V), lambda n, c: (row(n, c), 0)),
                   pl.BlockSpec((1, RET_HEADS, RET_DK, RET_DV), lambda n, c: (n, 0, 0, 0))],
        out_shape=[jax.ShapeDtypeStruct((batch * seq, RET_V), BF16),
                   jax.ShapeDtypeStruct((batch, RET_HEADS, RET_DK, RET_DV), F32)],
        scratch_shapes=[pltpu.VMEM((RET_HEADS, RET_DK, RET_DV), F32)],
        compiler_params=_params(32), name="retention_prompt")(proj, proj, proj, proj, gnw.reshape(1, RET_V))


def _ret_sample_kernel(q_ref, k_ref, v_ref, rg_ref, gnw_ref, st_in_ref, ar_ref, st_out_ref, o_scr, *, ts, ns_blk):
    i = pl.program_id(0)
    R = q_ref.shape[0]

    def sample_and_pos(shape, axis):
        r = lax.broadcasted_iota(I32, shape, axis).astype(F32)
        sid = jnp.floor((r + 0.5) * (1.0 / ts))
        return sid.astype(I32), r - sid * ts

    sid_r, t_r = sample_and_pos((R, R), 0)
    sid_c, t_c = sample_and_pos((R, R), 1)
    sid_k, t_k = sample_and_pos((R, RET_DK), 0)
    _, t_v = sample_and_pos((R, RET_DV), 0)
    for h in range(RET_HEADS):
        lg = _ret_log_decay(h)
        ck = slice(h * RET_DK, (h + 1) * RET_DK)
        cv = slice(h * RET_DV, (h + 1) * RET_DV)
        q = q_ref[:, ck]
        k = k_ref[:, ck] * (RET_DK ** -0.5)
        v = v_ref[:, cv].astype(BF16)

        @pl.when(i == 0)
        def _():
            s = lax.dot_general(q.astype(BF16), k.astype(BF16), _NT, preferred_element_type=F32)
            dt = t_r - t_c
            decay = jnp.where((sid_r == sid_c) & (dt >= 0), jnp.exp(lg * jnp.maximum(dt, 0.0)), 0.0)
            o_scr[:, cv] = jnp.dot((s * decay).astype(BF16), v, preferred_element_type=F32)

        q_dec = jnp.exp(lg * (t_v + 1.0))
        kd = k * jnp.exp(lg * (ts - 1.0 - t_k))
        for j in range(ns_blk):
            mine = sid_k == i * ns_blk + j
            st = st_in_ref[j, h]
            qm = jnp.where(mine, q, 0.0).astype(BF16)
            o_scr[:, cv] += jnp.dot(qm, st.astype(BF16), preferred_element_type=F32) * q_dec
            kdm = jnp.where(mine, kd, 0.0).T.astype(BF16)
            st_out_ref[j, h] = st * math.exp(lg * ts) + jnp.dot(kdm, v, preferred_element_type=F32)

    @pl.when(i == pl.num_programs(0) - 1)
    def _():
        for h in range(RET_HEADS):
            cv = slice(h * RET_DV, (h + 1) * RET_DV)
            ar_ref[:, cv] = _group_norm_gate(o_scr[:, cv], gnw_ref[:, cv], rg_ref[:, cv])


def _retention_sample(proj_s, gnw, state, ts):
    rows = proj_s.shape[0]
    ns = state.shape[0]
    ns_blk = _pick(ns, (4, 2, 1))
    return pl.pallas_call(
        functools.partial(_ret_sample_kernel, ts=ts, ns_blk=ns_blk), grid=(ns // ns_blk,),
        in_specs=[pl.BlockSpec((rows, RET_QK), lambda i: (0, OFF_RQ // RET_QK)),
                  pl.BlockSpec((rows, RET_QK), lambda i: (0, OFF_RK // RET_QK)),
                  pl.BlockSpec((rows, RET_V), lambda i: (0, OFF_RV // RET_V)),
                  pl.BlockSpec((rows, RET_V), lambda i: (0, OFF_RG // RET_V)),
                  pl.BlockSpec((1, RET_V), lambda i: (0, 0)),
                  pl.BlockSpec((ns_blk, RET_HEADS, RET_DK, RET_DV), lambda i: (i, 0, 0, 0))],
        out_specs=[pl.BlockSpec((rows, RET_V), lambda i: (0, 0)),
                   pl.BlockSpec((ns_blk, RET_HEADS, RET_DK, RET_DV), lambda i: (i, 0, 0, 0))],
        out_shape=[jax.ShapeDtypeStruct((rows, RET_V), BF16),
                   jax.ShapeDtypeStruct(state.shape, F32)],
        scratch_shapes=[pltpu.VMEM((rows, RET_V), F32)],
        compiler_params=_params(40), name="retention_sample")(
            proj_s, proj_s, proj_s, proj_s, gnw.reshape(1, RET_V), state)


def _alibi_slope(head):
    return 2.0 ** (-ALIBI_MAX_EXP * (head + 1.0) / DIL_HEADS)


def _dil_prompt_kernel(slope_ref, q_ref, k_ref, kp_ref, v_ref, vp_ref, o_ref, lse_ref, *, group, dil, win_steps, nq):
    j = pl.program_id(1)
    hh = pl.program_id(2)
    slope = slope_ref[group * DIL_HPG + hh]
    B = DIL_BLOCK
    qi = lax.broadcasted_iota(I32, (B, B), 0)
    kj = lax.broadcasted_iota(I32, (B, B), 1)
    steps_prev = qi + B - kj
    steps_cur = qi - kj
    in_prev = steps_prev <= win_steps
    in_prev_first = in_prev & (jnp.full((B, B), j, I32) > 0)
    in_cur = (steps_cur >= 0) & (steps_cur <= win_steps)
    bias_prev = slope * (steps_prev * dil).astype(F32)
    bias_cur = slope * (steps_cur * dil).astype(F32)
    scale = DIL_HD ** -0.5
    tiles = [(qb, r) for qb in range(nq) for r in range(dil)]

    def rows_of(qb, r):
        return pl.ds(qb * B * dil + r, B, stride=dil)

    def prev_of(ref, first_ref, qb, r):
        return first_ref[rows_of(0, r), :] if qb == 0 else ref[rows_of(qb - 1, r), :]

    def scores(qb, r):
        q = q_ref[rows_of(qb, r), :].astype(BF16)
        sp = lax.dot_general(q, prev_of(k_ref, kp_ref, qb, r).astype(BF16), _NT, preferred_element_type=F32)
        sc = lax.dot_general(q, k_ref[rows_of(qb, r), :].astype(BF16), _NT, preferred_element_type=F32)
        return sp, sc

    def softmax(qb, sp, sc):
        sp = jnp.where(in_prev_first if qb == 0 else in_prev, sp * scale - bias_prev, -jnp.inf)
        sc = jnp.where(in_cur, sc * scale - bias_cur, -jnp.inf)
        m = jnp.maximum(jnp.max(sp, axis=-1, keepdims=True), jnp.max(sc, axis=-1, keepdims=True))
        ep = jnp.exp(sp - m)
        ec = jnp.exp(sc - m)
        l = jnp.sum(ep, axis=-1, keepdims=True) + jnp.sum(ec, axis=-1, keepdims=True)
        return (ep / l).astype(BF16), (ec / l).astype(BF16), m + jnp.log(l)

    def finish(qb, r, pp, pc, lse):
        o = jnp.dot(pp, prev_of(v_ref, vp_ref, qb, r).astype(BF16), preferred_element_type=F32)
        o = o + jnp.dot(pc, v_ref[rows_of(qb, r), :].astype(BF16), preferred_element_type=F32)
        o_ref[rows_of(qb, r), :] = o
        lse_ref[rows_of(qb, r), :] = jnp.broadcast_to(lse, (B, DIL_HD))

    s_ready, p_ready = {}, {}
    lag_softmax, lag_finish = 2, 4
    for step in range(len(tiles) + lag_finish):
        if step < len(tiles):
            s_ready[step] = scores(*tiles[step])
        if 0 <= step - lag_softmax < len(tiles):
            p_ready[step - lag_softmax] = softmax(tiles[step - lag_softmax][0], *s_ready.pop(step - lag_softmax))
        if 0 <= step - lag_finish < len(tiles):
            finish(*tiles[step - lag_finish], *p_ready.pop(step - lag_finish))


def _dilated_prompt(proj, slopes, group, batch, seq):
    win, dil = DIL_GROUPS[group]
    span = DIL_BLOCK * dil
    assert seq % span == 0
    nq = _pick(seq // span, [n for n in (8, 4, 2, 1) if n * dil <= 16])
    nj = seq // (span * nq)
    col = lambda off: off // DIL_HD + group * DIL_HPG

    def cur(off):
        return pl.BlockSpec((span * nq, DIL_HD), lambda b, j, hh: (b * nj + j, col(off) + hh))

    def prev(off):
        return pl.BlockSpec((span, DIL_HD), lambda b, j, hh: (jnp.maximum((b * nj + j) * nq - 1, 0), col(off) + hh))

    out_spec = pl.BlockSpec((span * nq, DIL_HD), lambda b, j, hh: (b * nj + j, hh))
    return pl.pallas_call(
        functools.partial(_dil_prompt_kernel, group=group, dil=dil, win_steps=win // dil, nq=nq),
        grid=(batch, nj, DIL_HPG),
        in_specs=[pl.BlockSpec(memory_space=pltpu.SMEM), cur(OFF_DQ), cur(OFF_DK), prev(OFF_DK), cur(OFF_DV), prev(OFF_DV)],
        out_specs=[out_spec, out_spec],
        out_shape=[jax.ShapeDtypeStruct((batch * seq, DIL_W), F32)] * 2,
        compiler_params=_params(32), name=f"dilated_prompt_g{group}")(slopes, proj, proj, proj, proj, proj)


def _dil_sample_kernel(q_ref, kn_ref, vn_ref, kc_ref, vc_ref, o_ref, lse_ref, ok_ref, ov_ref,
                       *, group, dil, win_steps, ts):
    H = DIL_HPG
    buf = kc_ref.shape[0] // H
    P = q_ref.shape[0]
    t = lax.broadcasted_iota(I32, (P, buf), 0)
    r = lax.broadcasted_iota(I32, (P, buf), 1)
    diff = buf + t - r
    span = win_steps * dil
    ok = ((diff & (dil - 1)) == 0) & (diff <= span) & (diff >= 0)
    bias_steps = diff.astype(F32)
    t1 = lax.broadcasted_iota(I32, (P, 1), 0)
    scale = DIL_HD ** -0.5
    for hh in range(DIL_HPG):
        slope = _alibi_slope(group * DIL_HPG + hh)
        c = slice(hh * DIL_HD, (hh + 1) * DIL_HD)
        q = q_ref[:, c]
        head_rows = pl.ds(hh, buf, stride=H)
        s = lax.dot_general(q.astype(BF16), kc_ref[head_rows, :].astype(BF16), _NT, preferred_element_type=F32)
        s = jnp.where(ok, s * scale - slope * bias_steps, -jnp.inf)
        m = jnp.max(s, axis=-1, keepdims=True)
        s_new = []
        for rp in range(ts):
            d = t1 - rp
            ok_n = (d >= 0) & ((d & (dil - 1)) == 0) & (d <= span)
            new_row = slice(rp * H + hh, rp * H + hh + 1)
            sn = jnp.sum(q * kn_ref[new_row, :], axis=-1, keepdims=True) * scale - slope * d.astype(F32)
            sn = jnp.where(ok_n, sn, -jnp.inf)
            s_new.append(sn)
            m = jnp.maximum(m, sn)
        e = jnp.exp(s - m)
        e_new = [jnp.exp(sn - m) for sn in s_new]
        l = jnp.sum(e, axis=-1, keepdims=True)
        for en in e_new:
            l = l + en
        o = jnp.dot((e / l).astype(BF16), vc_ref[head_rows, :].astype(BF16), preferred_element_type=F32)
        for rp in range(ts):
            o = o + (e_new[rp] / l) * vn_ref[rp * H + hh:rp * H + hh + 1, :]
        o_ref[:, c] = o
        lse_ref[:, c] = jnp.broadcast_to(m + jnp.log(l), (P, DIL_HD))
    ok_ref[0:(buf - ts) * H, :] = kc_ref[ts * H:buf * H, :]
    ok_ref[(buf - ts) * H:buf * H, :] = kn_ref[...]
    ov_ref[0:(buf - ts) * H, :] = vc_ref[ts * H:buf * H, :]
    ov_ref[(buf - ts) * H:buf * H, :] = vn_ref[...]


def _dilated_sample(q8, kn, vn, cache_k, cache_v, layer, group, ts):
    win, dil = DIL_GROUPS[group]
    depth, ns, buf = cache_k.shape[:3]
    assert buf == win and buf % dil == 0, "cached window must hold exactly one full window"
    flat = lambda c: c.reshape(depth, ns, buf * DIL_HPG, DIL_HD)
    small = pl.BlockSpec((None, SAMPLE_PAD, DIL_W), lambda n: (n, 0, 0))
    new = pl.BlockSpec((None, ts * DIL_HPG, DIL_HD), lambda n: (n, 0, 0))
    big_in = pl.BlockSpec((None, None, buf * DIL_HPG, DIL_HD), lambda n: (layer, n, 0, 0))
    big_out = pl.BlockSpec((None, buf * DIL_HPG, DIL_HD), lambda n: (n, 0, 0))
    o, lse, k_out, v_out = pl.pallas_call(
        functools.partial(_dil_sample_kernel, group=group, dil=dil, win_steps=win // dil, ts=ts),
        grid=(ns,), in_specs=[small, new, new, big_in, big_in],
        out_specs=[small, small, big_out, big_out],
        out_shape=[jax.ShapeDtypeStruct((ns, SAMPLE_PAD, DIL_W), F32)] * 2
        + [jax.ShapeDtypeStruct((ns, buf * DIL_HPG, DIL_HD), F32)] * 2,
        compiler_params=_params(56), name=f"dilated_sample_g{group}")(q8, kn, vn, flat(cache_k), flat(cache_v))
    return o, lse, k_out.reshape(ns, buf, DIL_HPG, DIL_HD), v_out.reshape(ns, buf, DIL_HPG, DIL_HD)


def _combine_kernel(o1_ref, o2_ref, o3_ref, l1_ref, l2_ref, l3_ref, out_ref):
    l1, l2, l3 = l1_ref[...], l2_ref[...], l3_ref[...]
    m = jnp.maximum(jnp.maximum(l1, l2), l3)
    e1, e2, e3 = jnp.exp(l1 - m), jnp.exp(l2 - m), jnp.exp(l3 - m)
    z = e1 + e2 + e3
    out = (e1 / z) * o1_ref[...] + (e2 / z) * o2_ref[...] + (e3 / z) * o3_ref[...]
    out_ref[...] = out.astype(out_ref.dtype)


def _combine_groups(outs, lses):
    m, w = outs[0].shape
    tm = _pick(m, (1024, 512, 256, 128, 64, 8))
    spec = pl.BlockSpec((tm, w), lambda i: (i, 0))
    return pl.pallas_call(_combine_kernel, grid=(m // tm,), in_specs=[spec] * 6, out_specs=spec,
                          out_shape=jax.ShapeDtypeStruct((m, w), BF16),
                          compiler_params=_params(40), name="combine_groups")(*outs, *lses)


def _mem_attn_head(q, k, v):
    s = lax.dot_general(q.astype(BF16), k.astype(BF16), _NT, preferred_element_type=F32) * (MEM_HD ** -0.5)
    e = jnp.exp(s - jnp.max(s, axis=-1, keepdims=True))
    p = e / jnp.sum(e, axis=-1, keepdims=True)
    return jnp.dot(p.astype(BF16), v.astype(BF16), preferred_element_type=F32)


def _mem_attn_kernel(q_ref, k_ref, v_ref, o_ref):
    for h in range(MEM_HEADS):
        c = slice(h * MEM_HD, (h + 1) * MEM_HD)
        o_ref[:, c] = _mem_attn_head(q_ref[:, c], k_ref[:, c], v_ref[:, c]).astype(o_ref.dtype)


def _mem_attn_cached_kernel(q_ref, k_ref, v_ref, o_ref):
    chunks = MEM_HD // 128
    mem_len = k_ref.shape[0] // (MEM_HEADS * chunks)
    for h in range(MEM_HEADS):
        rows = [pl.ds(h * chunks + c, mem_len, stride=MEM_HEADS * chunks) for c in range(chunks)]
        cols = [slice(h * MEM_HD + c * 128, h * MEM_HD + (c + 1) * 128) for c in range(chunks)]
        s = sum(lax.dot_general(q_ref[:, cols[c]].astype(BF16), k_ref[rows[c], :].astype(BF16), _NT,
                                preferred_element_type=F32) for c in range(chunks)) * (MEM_HD ** -0.5)
        e = jnp.exp(s - jnp.max(s, axis=-1, keepdims=True))
        p = (e / jnp.sum(e, axis=-1, keepdims=True)).astype(BF16)
        for c in range(chunks):
            o_ref[:, cols[c]] = jnp.dot(p, v_ref[rows[c], :].astype(BF16),
                                        preferred_element_type=F32).astype(o_ref.dtype)


def _memory_attend_prompt(proj, mkv, batch, seq):
    mem_len = mkv.shape[1]
    tq = _pick(seq, (512, 256, 128))
    nq = seq // tq
    return pl.pallas_call(
        _mem_attn_kernel, grid=(batch, nq),
        in_specs=[pl.BlockSpec((tq, MEM_W), lambda b, i: (b * nq + i, OFF_MQ // MEM_W)),
                  pl.BlockSpec((None, mem_len, MEM_W), lambda b, i: (b, 0, 0)),
                  pl.BlockSpec((None, mem_len, MEM_W), lambda b, i: (b, 0, 1))],
        out_specs=pl.BlockSpec((tq, MEM_W), lambda b, i: (b * nq + i, 0)),
        out_shape=jax.ShapeDtypeStruct((batch * seq, MEM_W), BF16),
        compiler_params=_params(40), name="memory_attend_prompt")(proj, mkv, mkv)


def _memory_attend_sample(q8, mem_k, mem_v, layer):
    depth, ns, mem_len = mem_k.shape[:3]
    rows = mem_len * MEM_W // 128
    flat = lambda c: c.reshape(depth, ns, rows, 128)
    small = pl.BlockSpec((None, SAMPLE_PAD, MEM_W), lambda n: (n, 0, 0))
    big = pl.BlockSpec((None, None, rows, 128), lambda n: (layer, n, 0, 0))
    return pl.pallas_call(
        _mem_attn_cached_kernel, grid=(ns,), in_specs=[small, big, big], out_specs=small,
        out_shape=jax.ShapeDtypeStruct((ns, SAMPLE_PAD, MEM_W), F32),
        compiler_params=_params(24), name="memory_attend_sample")(q8, flat(mem_k), flat(mem_v))


def _mix_kernel(ar_ref, ad_ref, am_ref, wr_ref, wd_ref, wm_ref, gr_ref, gd_ref, gm_ref, o_ref):
    r = jnp.dot(ar_ref[...], wr_ref[...], preferred_element_type=F32)
    d = jnp.dot(ad_ref[...], wd_ref[...], preferred_element_type=F32)
    m = jnp.dot(am_ref[...], wm_ref[...], preferred_element_type=F32)
    mix = jax.nn.sigmoid(gr_ref[...]) * r + jax.nn.sigmoid(gd_ref[...]) * d + jax.nn.sigmoid(gm_ref[...]) * m
    o_ref[...] = mix.astype(o_ref.dtype)


def _branch_mix(a_r, a_d, a_m, w_r, w_d, w_m, proj, row0):
    t = a_r.shape[0]
    tm = _row_tile(t, row0, (1024, 832, 640, 512, 416, 320, 256, 128, 64, 8))
    blk0 = row0 // tm
    tn = 512
    nj = D_MODEL // tn
    lhs = lambda w: pl.BlockSpec((tm, w), lambda i, j: (i, 0))
    rhs = lambda w: pl.BlockSpec((w, tn), lambda i, j: (0, j))
    gate = lambda b: pl.BlockSpec((tm, tn), lambda i, j: (blk0 + i, OFF_GATE // tn + b * nj + j))
    return pl.pallas_call(
        _mix_kernel, grid=(t // tm, nj),
        in_specs=[lhs(RET_V), lhs(DIL_W), lhs(MEM_W), rhs(RET_V), rhs(DIL_W), rhs(MEM_W), gate(0), gate(1), gate(2)],
        out_specs=pl.BlockSpec((tm, tn), lambda i, j: (i, j)),
        out_shape=jax.ShapeDtypeStruct((t, D_MODEL), BF16),
        compiler_params=_params(56), name="branch_mix")(a_r, a_d, a_m, w_r, w_d, w_m, proj, proj, proj)


def _staircase(k):
    return [(a, k // (a + 1)) for a in range(k)]


def _peer_topk_kernel(q_ref, keys_ref, i_ref, j_ref, g_ref, s_scr, key_scr, cand_scr, pos_scr):
    tk = q_ref.shape[0]
    K = PEER_TOPK
    SUB = 8
    iota_k = lax.broadcasted_iota(I32, (K, tk), 0)

    def top_k_rows(score_refs, id_ref):
        def body(k, carry):
            sel = iota_k == k
            out = []
            for ref, (prev, top_s, top_i) in zip(score_refs, carry):
                best = best_id = None
                for v in range(ref.shape[0] // SUB):
                    rows = slice(v * SUB, (v + 1) * SUB)
                    ids = id_ref[rows, :]
                    sv = jnp.where(ids == prev, -jnp.inf, ref[rows, :])
                    ref[rows, :] = sv
                    if best is None:
                        best, best_id = sv, ids
                    else:
                        best_id = jnp.where(sv > best, ids, best_id)
                        best = jnp.maximum(best, sv)
                m = jnp.max(best, axis=0, keepdims=True)
                pos = jnp.min(jnp.where(best == m, best_id, jnp.iinfo(jnp.int32).max), axis=0, keepdims=True)
                out.append((pos, jnp.where(sel, m, top_s), jnp.where(sel, pos, top_i)))
            return tuple(out)

        init = tuple((jnp.full((1, tk), -1, I32), jnp.zeros((K, tk), F32), jnp.zeros((K, tk), I32))
                     for _ in score_refs)
        return [(top_s, top_i) for _, top_s, top_i in lax.fori_loop(0, K, body, init)]

    for c in range(2):
        qc = q_ref[:, c * PEER_NKEYS:(c + 1) * PEER_NKEYS].astype(BF16)
        s_scr[c] = lax.dot_general(keys_ref[0, c].astype(BF16), qc, _NT, preferred_element_type=F32)
    key_scr[...] = lax.broadcasted_iota(I32, key_scr.shape, 0)
    (s1, i1), (s2, i2) = top_k_rows([s_scr.at[0], s_scr.at[1]], key_scr)

    cand_scr[...] = jnp.full(cand_scr.shape, -jnp.inf, F32)
    pos_scr[...] = jnp.full(pos_scr.shape, K * K, I32)
    off = 0
    for a, nb in _staircase(K):
        cand_scr[off:off + nb, :] = s1[a:a + 1, :] + s2[0:nb, :]
        pos_scr[off:off + nb, :] = lax.broadcasted_iota(I32, (nb, tk), 0) + a * K
        off += nb
    (best_s, best_p), = top_k_rows([cand_scr], pos_scr)
    pa = jnp.right_shift(best_p, K.bit_length() - 1)
    pb = best_p - pa * K
    ei = jnp.zeros((K, tk), I32)
    ej = jnp.zeros((K, tk), I32)
    for a in range(K):
        ei = jnp.where(pa == a, i1[a:a + 1, :], ei)
        ej = jnp.where(pb == a, i2[a:a + 1, :], ej)
    e = jnp.exp(best_s - jnp.max(best_s, axis=0, keepdims=True))
    i_ref[...] = ei.astype(F32)
    j_ref[...] = ej.astype(F32)
    g_ref[...] = e / jnp.sum(e, axis=0, keepdims=True)


def _peer_route(q, sub_keys):
    t = q.shape[0]
    tk = _pick(t, (640, 512, 256, 128))
    n_cand = -(-sum(nb for _, nb in _staircase(PEER_TOPK)) // 8) * 8
    out_spec = pl.BlockSpec((PEER_TOPK, tk), lambda i, h: (h, i))
    return pl.pallas_call(
        _peer_topk_kernel, grid=(t // tk, PEER_HEADS),
        in_specs=[pl.BlockSpec((tk, PEER_DQ), lambda i, h: (i, h)),
                  pl.BlockSpec((1, 2, PEER_NKEYS, PEER_DQ // 2), lambda i, h: (h, 0, 0, 0))],
        out_specs=[out_spec] * 3,
        out_shape=[jax.ShapeDtypeStruct((PEER_HEADS * PEER_TOPK, t), F32)] * 3,
        scratch_shapes=[pltpu.VMEM((2, PEER_NKEYS, tk), F32), pltpu.VMEM((PEER_NKEYS, tk), I32),
                        pltpu.VMEM((n_cand, tk), F32), pltpu.VMEM((n_cand, tk), I32)],
        compiler_params=_params(24), name="peer_route")(q, sub_keys)


def _peer_wbuild_kernel(i_ref, j_ref, g_ref, w_ref, it_scr, jt_scr, gt_scr, wt_scr):
    tw = w_ref.shape[0]
    it_scr[...] = i_ref[...].T
    jt_scr[...] = j_ref[...].T
    gt_scr[...] = g_ref[...].T
    n = PEER_NKEYS
    iota_r = lax.broadcasted_iota(I32, (n, i_ref.shape[0]), 0).astype(F32)

    def body(t, carry):
        irow = it_scr[pl.ds(t, 1), :]
        jrow = jt_scr[pl.ds(t, 1), :]
        grow = gt_scr[pl.ds(t, 1), :]
        a_t = jnp.where(iota_r == irow, grow, 0.0).astype(BF16)
        b_t = jnp.where(iota_r == jrow, 1.0, 0.0).astype(BF16)
        w = lax.dot_general(a_t, b_t, _NT, preferred_element_type=F32)
        wt_scr[pl.ds(pl.multiple_of(t * W_ROW_PITCH, 8), n), :] = w
        return carry

    lax.fori_loop(0, tw, body, 0, unroll=64)
    for i in range(n):
        w_ref[:, i * n:(i + 1) * n] = wt_scr[pl.ds(i, tw, stride=W_ROW_PITCH), :].astype(BF16)


def _peer_wbuild(ei, ej, g):
    nsel, t = ei.shape
    tw = 128
    in_spec = pl.BlockSpec((nsel, tw), lambda i: (0, i))
    n = PEER_NKEYS
    return pl.pallas_call(
        _peer_wbuild_kernel, grid=(t // tw,), in_specs=[in_spec] * 3,
        out_specs=pl.BlockSpec((tw, n * n), lambda i: (i, 0)),
        out_shape=jax.ShapeDtypeStruct((t, n * n), BF16),
        scratch_shapes=[pltpu.VMEM((tw, nsel), F32)] * 3 + [pltpu.VMEM((tw * W_ROW_PITCH, n), F32)],
        compiler_params=_params(32), name="peer_wbuild")(ei, ej, g)


def _gelu_tanh(x):
    return x * (0.5 * (1.0 + jnp.tanh(math.sqrt(2.0 / math.pi) * (x + 0.044715 * (x * x * x)))))


def _peer_mlp_kernel(x_ref, u_ref, v_ref, w_ref, o_ref):
    e = pl.program_id(1)

    @pl.when(e == 0)
    def _():
        o_ref[...] = jnp.zeros_like(o_ref)

    half = u_ref.shape[0] // 2
    ps = []
    for c in range(2):
        rows = slice(c * half, (c + 1) * half)
        act = lax.dot_general(x_ref[...], u_ref[rows, :], _NT, preferred_element_type=F32)
        ps.append((w_ref[:, rows].astype(F32) * _gelu_tanh(act)).astype(BF16))
    p = jnp.concatenate(ps, axis=1)
    o_ref[...] = o_ref[...] + jnp.dot(p, v_ref[...], preferred_element_type=F32)


def _peer_mlp(n2, u, v, w):
    t, d = n2.shape
    ne = u.shape[0]
    tm = _pick(t, (640, 416, 320, 256, 128, 64, 8))
    eb = 512
    return pl.pallas_call(
        _peer_mlp_kernel, grid=(t // tm, ne // eb),
        in_specs=[pl.BlockSpec((tm, d), lambda i, e: (i, 0)),
                  pl.BlockSpec((eb, d), lambda i, e: (e, 0)),
                  pl.BlockSpec((eb, d), lambda i, e: (e, 0)),
                  pl.BlockSpec((tm, eb), lambda i, e: (i, e))],
        out_specs=pl.BlockSpec((tm, d), lambda i, e: (i, 0)),
        out_shape=jax.ShapeDtypeStruct((t, d), F32),
        compiler_params=_params(62), name="peer_mlp")(n2, u, v, w)


def _pad_sample_rows(a, ns, ts):
    a = a.reshape(ns, ts, a.shape[-1])
    return jnp.pad(a, ((0, 0), (0, SAMPLE_PAD - ts), (0, 0)))


def kernel(x_prompt, x_sample, mem_prompt, cache_ret_state, cache_win_k1, cache_win_v1, cache_win_k2, cache_win_v2, cache_win_k3, cache_win_v3, cache_mem_k, cache_mem_v, norm1_w, w_in, ret_gn_w, w_ret_o, w_dil_o, mem_norm_w, w_mem_kv, w_mem_o, w_out, norm2_w, w_peer_q, peer_sub_keys, peer_u, peer_v, final_norm_w):
    batch, seq, d = x_prompt.shape
    ns, ts, _ = x_sample.shape
    mem_len = mem_prompt.shape[1]
    depth = norm1_w.shape[0]
    assert d == D_MODEL and seq % RET_CHUNK == 0 and ts <= SAMPLE_PAD
    tp, tsamp = batch * seq, ns * ts
    win_k = (cache_win_k1, cache_win_k2, cache_win_k3)
    win_v = (cache_win_v1, cache_win_v2, cache_win_v3)

    slopes = jnp.asarray([_alibi_slope(h) for h in range(DIL_HEADS)], F32)
    x_p, x_s = x_prompt.reshape(tp, d), x_sample.reshape(tsamp, d)
    prompt_states, sample_states = [], []
    for l in range(depth):
        proj = _matmul(_rmsnorm_pair(x_p, x_s, norm1_w[l], BF16), w_in[l],
                       tm_cands=(1664, 832, 640, 512, 416, 320, 256, 128, 64, 8), name="input_projection")
        proj_s = proj[tp:]

        ar_p, ret_p = _retention_prompt(proj, ret_gn_w[l], batch, seq)
        ar_s, ret_s = _retention_sample(proj_s, ret_gn_w[l], cache_ret_state[l], ts)

        outs_p, lses_p, outs_s, lses_s, bufs_p, bufs_s = [], [], [], [], [], []
        for g, (win, dil) in enumerate(DIL_GROUPS):
            o, lse = _dilated_prompt(proj, slopes, g, batch, seq)
            outs_p.append(o)
            lses_p.append(lse)
            keep = min(win, seq)
            for off in (OFF_DK, OFF_DV):
                c0 = off + g * DIL_W
                tail = [proj[b * seq + seq - keep:(b + 1) * seq, c0:c0 + DIL_W] for b in range(batch)]
                bufs_p.append(jnp.stack(tail, axis=0).reshape(batch, keep, DIL_HPG, DIL_HD))
            q_s, k_s, v_s = (proj_s[:, off + g * DIL_W: off + (g + 1) * DIL_W] for off in (OFF_DQ, OFF_DK, OFF_DV))
            o8, lse8, k_new, v_new = _dilated_sample(
                _pad_sample_rows(q_s, ns, ts), k_s.reshape(ns, ts * DIL_HPG, DIL_HD), v_s.reshape(ns, ts * DIL_HPG, DIL_HD),
                win_k[g], win_v[g], l, g, ts)
            outs_s.append(o8[:, :ts].reshape(tsamp, DIL_W))
            lses_s.append(lse8[:, :ts].reshape(tsamp, DIL_W))
            bufs_s += [k_new, v_new]
        ad_p = _combine_groups(outs_p, lses_p)
        ad_s = _combine_groups(outs_s, lses_s)

        mem_n = _rmsnorm(mem_prompt.reshape(batch * mem_len, d), mem_norm_w[l], BF16)
        mkv = _matmul(mem_n, w_mem_kv[l].astype(BF16), name="memory_kv").reshape(batch, mem_len, 2 * MEM_W)
        am_p = _memory_attend_prompt(proj, mkv, batch, seq)
        mq8 = _pad_sample_rows(proj_s[:, OFF_MQ:OFF_MQ + MEM_W], ns, ts)
        am_s = _memory_attend_sample(mq8, cache_mem_k, cache_mem_v, l)
        am_s = am_s[:, :ts].reshape(tsamp, MEM_W).astype(BF16)

        branch_w = (w_ret_o[l].astype(BF16), w_dil_o[l].astype(BF16), w_mem_o[l].astype(BF16))
        w_out_b = w_out[l].astype(BF16)
        h_p = _matmul(_branch_mix(ar_p, ad_p, am_p, *branch_w, proj, 0), w_out_b, res=x_p, name="output_projection")
        h_s = _matmul(_branch_mix(ar_s, ad_s, am_s, *branch_w, proj, tp), w_out_b, res=x_s, name="output_projection")

        n2 = _rmsnorm_pair(h_p, h_s, norm2_w[l], BF16)
        pq = _matmul(n2, w_peer_q[l].astype(BF16), name="peer_query")
        ei, ej, gw = _peer_route(pq, peer_sub_keys[l])
        w_dense = _peer_wbuild(ei, ej, gw)
        peer = _peer_mlp(n2, peer_u[l].astype(BF16), peer_v[l].astype(BF16), w_dense)

        mk = mkv[:, :, :MEM_W].reshape(batch, mem_len, MEM_HEADS, MEM_HD)
        mv = mkv[:, :, MEM_W:].reshape(batch, mem_len, MEM_HEADS, MEM_HD)
        prompt_states.append((ret_p, *bufs_p, mk, mv))
        sample_states.append((ret_s, *bufs_s))
        if l + 1 < depth:
            x_p, x_s = _add(h_p, peer, 0), _add(h_s, peer, tp)

    y_prompt = _add_rmsnorm(h_p, peer, final_norm_w, 0).reshape(batch, seq, d)
    y_sample = _add_rmsnorm(h_s, peer, final_norm_w, tp).reshape(ns, ts, d)
    p_out = [jnp.stack(s, axis=0) for s in zip(*prompt_states)]
    s_out = [jnp.stack(s, axis=0) for s in zip(*sample_states)]
    return (y_prompt, y_sample, *p_out, *s_out)
```

```python
import functools
import math

import jax
import jax.numpy as jnp
from jax import lax
from jax.experimental import pallas as pl
from jax.experimental.pallas import tpu as pltpu

F32, BF16, I32 = jnp.float32, jnp.bfloat16, jnp.int32

D_MODEL = 4096
RET_HEADS, RET_DK, RET_DV, RET_CHUNK = 8, 128, 256, 128
DIL_GROUPS = ((128, 1), (512, 4), (2048, 16))
DIL_HPG, DIL_HD, DIL_BLOCK = 4, 128, 128
DIL_HEADS = DIL_HPG * len(DIL_GROUPS)
ALIBI_MAX_EXP = 8.0
MEM_HEADS, MEM_HD = 4, 384
PEER_HEADS, PEER_NKEYS, PEER_TOPK, PEER_DQ = 8, 128, 16, 256
NORM_EPS = 1e-6

RET_QK = RET_HEADS * RET_DK
RET_V = RET_HEADS * RET_DV
DIL_W = DIL_HPG * DIL_HD
DIL_ALL = DIL_HEADS * DIL_HD
MEM_W = MEM_HEADS * MEM_HD
OFF_RQ = 0
OFF_RK = OFF_RQ + RET_QK
OFF_RV = OFF_RK + RET_QK
OFF_RG = OFF_RV + RET_V
OFF_DQ = OFF_RG + RET_V
OFF_DK = OFF_DQ + DIL_ALL
OFF_DV = OFF_DK + DIL_ALL
OFF_MQ = OFF_DV + DIL_ALL
OFF_GATE = OFF_MQ + MEM_W
PROJ_W = OFF_GATE + 3 * D_MODEL

SAMPLE_PAD = 8
W_ROW_PITCH = PEER_NKEYS + 8
V7X_VMEM_BYTES = 64 * 1024 * 1024
MIB = 1024 * 1024

_NT = (((1,), (1,)), ((), ()))


def _pick(n, cands):
    for c in cands:
        if n % c == 0:
            return c
    raise ValueError(f"no tile in {cands} divides {n}")


def _params(vmem_mib):
    return pltpu.CompilerParams(vmem_limit_bytes=min(vmem_mib * MIB, V7X_VMEM_BYTES - 2 * MIB))


def _rmsnorm_kernel(x_ref, w_ref, o_ref):
    x = x_ref[...]
    ms = jnp.mean(x * x, axis=-1, keepdims=True)
    o_ref[...] = (x * lax.rsqrt(ms + NORM_EPS) * w_ref[...]).astype(o_ref.dtype)


def _rmsnorm(x, w, out_dtype):
    m, d = x.shape
    tm = _pick(m, (320, 256, 128, 64, 8))
    return pl.pallas_call(
        _rmsnorm_kernel, grid=(m // tm,),
        in_specs=[pl.BlockSpec((tm, d), lambda i: (i, 0)), pl.BlockSpec((1, d), lambda i: (0, 0))],
        out_specs=pl.BlockSpec((tm, d), lambda i: (i, 0)),
        out_shape=jax.ShapeDtypeStruct((m, d), out_dtype),
        compiler_params=_params(48), name="rmsnorm")(x, w.reshape(1, d))


def _add_rmsnorm_kernel(a_ref, b_ref, w_ref, o_ref):
    x = a_ref[...] + b_ref[...]
    ms = jnp.mean(x * x, axis=-1, keepdims=True)
    o_ref[...] = x * lax.rsqrt(ms + NORM_EPS) * w_ref[...]


def _rmsnorm_pair_kernel(a_ref, b_ref, w_ref, o_ref, *, na):
    i = pl.program_id(0)

    def norm(x):
        ms = jnp.mean(x * x, axis=-1, keepdims=True)
        return (x * lax.rsqrt(ms + NORM_EPS) * w_ref[...]).astype(o_ref.dtype)

    @pl.when(i < na)
    def _():
        o_ref[...] = norm(a_ref[...])

    @pl.when(i >= na)
    def _():
        o_ref[0:b_ref.shape[0], :] = norm(b_ref[...])


def _rmsnorm_pair(a, b, w, out_dtype):
    d = a.shape[1]
    tm = _pick(a.shape[0], (512, 256, 128, 64, 8))
    assert b.shape[0] <= tm and b.shape[0] % 8 == 0
    na = a.shape[0] // tm
    return pl.pallas_call(
        functools.partial(_rmsnorm_pair_kernel, na=na), grid=(na + 1,),
        in_specs=[pl.BlockSpec((tm, d), lambda i: (jnp.minimum(i, na - 1), 0)),
                  pl.BlockSpec(b.shape, lambda i: (0, 0)),
                  pl.BlockSpec((1, d), lambda i: (0, 0))],
        out_specs=pl.BlockSpec((tm, d), lambda i: (i, 0)),
        out_shape=jax.ShapeDtypeStruct((a.shape[0] + b.shape[0], d), out_dtype),
        compiler_params=_params(48), name="rmsnorm_stream")(a, b, w.reshape(1, d))


def _row_tile(rows, row0, cands):
    return _pick(math.gcd(row0, rows) if row0 else rows, cands)


def _add_rmsnorm(a, b, w, b_row0):
    rows, d = a.shape
    tm = _row_tile(rows, b_row0, (256, 128, 64, 8))
    blk0 = b_row0 // tm
    return pl.pallas_call(
        _add_rmsnorm_kernel, grid=(rows // tm,),
        in_specs=[pl.BlockSpec((tm, d), lambda i: (i, 0)), pl.BlockSpec((tm, d), lambda i: (blk0 + i, 0)),
                  pl.BlockSpec((1, d), lambda i: (0, 0))],
        out_specs=pl.BlockSpec((tm, d), lambda i: (i, 0)),
        out_shape=jax.ShapeDtypeStruct((rows, d), F32),
        compiler_params=_params(56), name="add_rmsnorm")(a, b, w.reshape(1, d))


def _add_kernel(a_ref, b_ref, o_ref):
    o_ref[...] = a_ref[...] + b_ref[...]


def _add(a, b, b_row0):
    rows, d = a.shape
    tm = _row_tile(rows, b_row0, (256, 128, 64, 8))
    blk0 = b_row0 // tm
    spec = pl.BlockSpec((tm, d), lambda i: (i, 0))
    return pl.pallas_call(_add_kernel, grid=(rows // tm,),
                          in_specs=[spec, pl.BlockSpec((tm, d), lambda i: (blk0 + i, 0))], out_specs=spec,
                          out_shape=jax.ShapeDtypeStruct((rows, d), F32),
                          compiler_params=_params(56), name="residual_add")(a, b)


def _mm_kernel(a_ref, b_ref, o_ref):
    o_ref[...] = jnp.dot(a_ref[...], b_ref[...].astype(a_ref.dtype), preferred_element_type=F32).astype(o_ref.dtype)


def _mm_res_kernel(a_ref, b_ref, r_ref, o_ref):
    o_ref[...] = r_ref[...] + jnp.dot(a_ref[...], b_ref[...], preferred_element_type=F32)


def _matmul(a, b, res=None, tm_cands=(1024, 832, 640, 512, 416, 320, 256, 128, 64, 8), tn=512, name="matmul"):
    m, k = a.shape
    n = b.shape[1]
    tm = _pick(m, tm_cands)
    tn = _pick(n, (tn, 256, 128))
    in_specs = [pl.BlockSpec((tm, k), lambda i, j: (i, 0)), pl.BlockSpec((k, tn), lambda i, j: (0, j))]
    args = [a, b]
    body = _mm_kernel
    if res is not None:
        in_specs.append(pl.BlockSpec((tm, tn), lambda i, j: (i, j)))
        args.append(res)
        body = _mm_res_kernel
    return pl.pallas_call(
        body, grid=(m // tm, n // tn), in_specs=in_specs,
        out_specs=pl.BlockSpec((tm, tn), lambda i, j: (i, j)),
        out_shape=jax.ShapeDtypeStruct((m, n), F32),
        compiler_params=_params(56), name=name)(*args)


def _ret_log_decay(h):
    return math.log(1.0 - 2.0 ** (-5.0 - h))


def _group_norm_gate(o, gnw, rg):
    mu = jnp.mean(o, axis=-1, keepdims=True)
    d = o - mu
    var = jnp.mean(d * d, axis=-1, keepdims=True)
    y = d * lax.rsqrt(var + NORM_EPS) * gnw
    return (y * (rg * jax.nn.sigmoid(rg))).astype(BF16)


def _ret_prompt_kernel(q_ref, k_ref, v_ref, rg_ref, gnw_ref, ar_ref, st_ref, state_scr):
    c = pl.program_id(1)
    L = RET_CHUNK

    @pl.when(c == 0)
    def _():
        state_scr[...] = jnp.zeros_like(state_scr)

    diff = (lax.broadcasted_iota(I32, (L, L), 0) - lax.broadcasted_iota(I32, (L, L), 1)).astype(F32)
    row_k = lax.broadcasted_iota(I32, (L, RET_DK), 0).astype(F32)
    row_v = lax.broadcasted_iota(I32, (L, RET_DV), 0).astype(F32)
    def first_matmuls(h):
        lg = _ret_log_decay(h)
        ck = slice(h * RET_DK, (h + 1) * RET_DK)
        q = q_ref[:, ck].astype(BF16)
        k = k_ref[:, ck] * (RET_DK ** -0.5)
        v = v_ref[:, h * RET_DV:(h + 1) * RET_DV].astype(BF16)
        s = lax.dot_general(q, k.astype(BF16), _NT, preferred_element_type=F32)
        st = state_scr[h]
        o_cross = jnp.dot(q, st.astype(BF16), preferred_element_type=F32)
        kd = (k * jnp.exp(lg * (L - 1.0 - row_k))).T.astype(BF16)
        state_scr[h] = st * math.exp(lg * L) + jnp.dot(kd, v, preferred_element_type=F32)
        return s, o_cross, v

    def decay_scores(h, s, o_cross, v):
        lg = _ret_log_decay(h)
        decay = jnp.where(diff >= 0, jnp.exp(lg * jnp.maximum(diff, 0.0)), 0.0)
        return (s * decay).astype(BF16), o_cross * jnp.exp(lg * (row_v + 1.0)), v

    def finish(h, inner, o_cross, v):
        cv = slice(h * RET_DV, (h + 1) * RET_DV)
        o = jnp.dot(inner, v, preferred_element_type=F32) + o_cross
        ar_ref[:, cv] = _group_norm_gate(o, gnw_ref[:, cv], rg_ref[:, cv])

    stage1, stage2 = {}, {}
    for step in range(RET_HEADS + 2):
        if step < RET_HEADS:
            stage1[step] = first_matmuls(step)
        if 0 <= step - 1 < RET_HEADS:
            stage2[step - 1] = decay_scores(step - 1, *stage1.pop(step - 1))
        if step - 2 >= 0:
            finish(step - 2, *stage2.pop(step - 2))

    @pl.when(c == pl.num_programs(1) - 1)
    def _():
        st_ref[0] = state_scr[...]


def _retention_prompt(proj, gnw, batch, seq):
    nc = seq // RET_CHUNK
    row = lambda n, c: n * nc + c
    return pl.pallas_call(
        _ret_prompt_kernel, grid=(batch, nc),
        in_specs=[pl.BlockSpec((RET_CHUNK, RET_QK), lambda n, c: (row(n, c), OFF_RQ // RET_QK)),
                  pl.BlockSpec((RET_CHUNK, RET_QK), lambda n, c: (row(n, c), OFF_RK // RET_QK)),
                  pl.BlockSpec((RET_CHUNK, RET_V), lambda n, c: (row(n, c), OFF_RV // RET_V)),
                  pl.BlockSpec((RET_CHUNK, RET_V), lambda n, c: (row(n, c), OFF_RG // RET_V)),
                  pl.BlockSpec((1, RET_V), lambda n, c: (0, 0))],
        out_specs=[pl.BlockSpec((RET_CHUNK, RET_V), lambda n, c: (row(n, c), 0)),
                   pl.BlockSpec((1, RET_HEADS, RET_DK, RET_DV), lambda n, c: (n, 0, 0, 0))],
        out_shape=[jax.ShapeDtypeStruct((batch * seq, RET_V), BF16),
                   jax.ShapeDtypeStruct((batch, RET_HEADS, RET_DK, RET_DV), F32)],
        scratch_shapes=[pltpu.VMEM((RET_HEADS, RET_DK, RET_DV), F32)],
        compiler_params=_params(32), name="retention_prompt")(proj, proj, proj, proj, gnw.reshape(1, RET_V))


def _ret_sample_kernel(q_ref, k_ref, v_ref, rg_ref, gnw_ref, st_in_ref, ar_ref, st_out_ref, o_scr, *, ts, ns_blk):
    i = pl.program_id(0)
    R = q_ref.shape[0]

    def sample_and_pos(shape, axis):
        r = lax.broadcasted_iota(I32, shape, axis).astype(F32)
        sid = jnp.floor((r + 0.5) * (1.0 / ts))
        return sid.astype(I32), r - sid * ts

    sid_r, t_r = sample_and_pos((R, R), 0)
    sid_c, t_c = sample_and_pos((R, R), 1)
    sid_k, t_k = sample_and_pos((R, RET_DK), 0)
    _, t_v = sample_and_pos((R, RET_DV), 0)
    for h in range(RET_HEADS):
        lg = _ret_log_decay(h)
        ck = slice(h * RET_DK, (h + 1) * RET_DK)
        cv = slice(h * RET_DV, (h + 1) * RET_DV)
        q = q_ref[:, ck]
        k = k_ref[:, ck] * (RET_DK ** -0.5)
        v = v_ref[:, cv].astype(BF16)

        @pl.when(i == 0)
        def _():
            s = lax.dot_general(q.astype(BF16), k.astype(BF16), _NT, preferred_element_type=F32)
            dt = t_r - t_c
            decay = jnp.where((sid_r == sid_c) & (dt >= 0), jnp.exp(lg * jnp.maximum(dt, 0.0)), 0.0)
            o_scr[:, cv] = jnp.dot((s * decay).astype(BF16), v, preferred_element_type=F32)

        q_dec = jnp.exp(lg * (t_v + 1.0))
        kd = k * jnp.exp(lg * (ts - 1.0 - t_k))
        for j in range(ns_blk):
            mine = sid_k == i * ns_blk + j
            st = st_in_ref[j, h]
            qm = jnp.where(mine, q, 0.0).astype(BF16)
            o_scr[:, cv] += jnp.dot(qm, st.astype(BF16), preferred_element_type=F32) * q_dec
            kdm = jnp.where(mine, kd, 0.0).T.astype(BF16)
            st_out_ref[j, h] = st * math.exp(lg * ts) + jnp.dot(kdm, v, preferred_element_type=F32)

    @pl.when(i == pl.num_programs(0) - 1)
    def _():
        for h in range(RET_HEADS):
            cv = slice(h * RET_DV, (h + 1) * RET_DV)
            ar_ref[:, cv] = _group_norm_gate(o_scr[:, cv], gnw_ref[:, cv], rg_ref[:, cv])


def _retention_sample(proj_s, gnw, state, ts):
    rows = proj_s.shape[0]
    ns = state.shape[0]
    ns_blk = _pick(ns, (4, 2, 1))
    return pl.pallas_call(
        functools.partial(_ret_sample_kernel, ts=ts, ns_blk=ns_blk), grid=(ns // ns_blk,),
        in_specs=[pl.BlockSpec((rows, RET_QK), lambda i: (0, OFF_RQ // RET_QK)),
                  pl.BlockSpec((rows, RET_QK), lambda i: (0, OFF_RK // RET_QK)),
                  pl.BlockSpec((rows, RET_V), lambda i: (0, OFF_RV // RET_V)),
                  pl.BlockSpec((rows, RET_V), lambda i: (0, OFF_RG // RET_V)),
                  pl.BlockSpec((1, RET_V), lambda i: (0, 0)),
                  pl.BlockSpec((ns_blk, RET_HEADS, RET_DK, RET_DV), lambda i: (i, 0, 0, 0))],
        out_specs=[pl.BlockSpec((rows, RET_V), lambda i: (0, 0)),
                   pl.BlockSpec((ns_blk, RET_HEADS, RET_DK, RET_DV), lambda i: (i, 0, 0, 0))],
        out_shape=[jax.ShapeDtypeStruct((rows, RET_V), BF16),
                   jax.ShapeDtypeStruct(state.shape, F32)],
        scratch_shapes=[pltpu.VMEM((rows, RET_V), F32)],
        compiler_params=_params(40), name="retention_sample")(
            proj_s, proj_s, proj_s, proj_s, gnw.reshape(1, RET_V), state)


def _alibi_slope(head):
    return 2.0 ** (-ALIBI_MAX_EXP * (head + 1.0) / DIL_HEADS)


def _dil_prompt_kernel(slope_ref, q_ref, k_ref, kp_ref, v_ref, vp_ref, o_ref, lse_ref, *, group, dil, win_steps, nq):
    j = pl.program_id(1)
    hh = pl.program_id(2)
    slope = slope_ref[group * DIL_HPG + hh]
    B = DIL_BLOCK
    qi = lax.broadcasted_iota(I32, (B, B), 0)
    kj = lax.broadcasted_iota(I32, (B, B), 1)
    steps_prev = qi + B - kj
    steps_cur = qi - kj
    in_prev = steps_prev <= win_steps
    in_prev_first = in_prev & (jnp.full((B, B), j, I32) > 0)
    in_cur = (steps_cur >= 0) & (steps_cur <= win_steps)
    bias_prev = slope * (steps_prev * dil).astype(F32)
    bias_cur = slope * (steps_cur * dil).astype(F32)
    scale = DIL_HD ** -0.5
    tiles = [(qb, r) for qb in range(nq) for r in range(dil)]

    def rows_of(qb, r):
        return pl.ds(qb * B * dil + r, B, stride=dil)

    def prev_of(ref, first_ref, qb, r):
        return first_ref[rows_of(0, r), :] if qb == 0 else ref[rows_of(qb - 1, r), :]

    def scores(qb, r):
        q = q_ref[rows_of(qb, r), :].astype(BF16)
        sp = lax.dot_general(q, prev_of(k_ref, kp_ref, qb, r).astype(BF16), _NT, preferred_element_type=F32)
        sc = lax.dot_general(q, k_ref[rows_of(qb, r), :].astype(BF16), _NT, preferred_element_type=F32)
        return sp, sc

    def softmax(qb, sp, sc):
        sp = jnp.where(in_prev_first if qb == 0 else in_prev, sp * scale - bias_prev, -jnp.inf)
        sc = jnp.where(in_cur, sc * scale - bias_cur, -jnp.inf)
        m = jnp.maximum(jnp.max(sp, axis=-1, keepdims=True), jnp.max(sc, axis=-1, keepdims=True))
        ep = jnp.exp(sp - m)
        ec = jnp.exp(sc - m)
        l = jnp.sum(ep, axis=-1, keepdims=True) + jnp.sum(ec, axis=-1, keepdims=True)
        return (ep / l).astype(BF16), (ec / l).astype(BF16), m + jnp.log(l)

    def finish(qb, r, pp, pc, lse):
        o = jnp.dot(pp, prev_of(v_ref, vp_ref, qb, r).astype(BF16), preferred_element_type=F32)
        o = o + jnp.dot(pc, v_ref[rows_of(qb, r), :].astype(BF16), preferred_element_type=F32)
        o_ref[rows_of(qb, r), :] = o
        lse_ref[rows_of(qb, r), :] = jnp.broadcast_to(lse, (B, DIL_HD))

    s_ready, p_ready = {}, {}
    lag_softmax, lag_finish = 2, 4
    for step in range(len(tiles) + lag_finish):
        if step < len(tiles):
            s_ready[step] = scores(*tiles[step])
        if 0 <= step - lag_softmax < len(tiles):
            p_ready[step - lag_softmax] = softmax(tiles[step - lag_softmax][0], *s_ready.pop(step - lag_softmax))
        if 0 <= step - lag_finish < len(tiles):
            finish(*tiles[step - lag_finish], *p_ready.pop(step - lag_finish))


def _dilated_prompt(proj, slopes, group, batch, seq):
    win, dil = DIL_GROUPS[group]
    span = DIL_BLOCK * dil
    assert seq % span == 0
    nq = _pick(seq // span, [n for n in (8, 4, 2, 1) if n * dil <= 16])
    nj = seq // (span * nq)
    col = lambda off: off // DIL_HD + group * DIL_HPG

    def cur(off):
        return pl.BlockSpec((span * nq, DIL_HD), lambda b, j, hh: (b * nj + j, col(off) + hh))

    def prev(off):
        return pl.BlockSpec((span, DIL_HD), lambda b, j, hh: (jnp.maximum((b * nj + j) * nq - 1, 0), col(off) + hh))

    out_spec = pl.BlockSpec((span * nq, DIL_HD), lambda b, j, hh: (b * nj + j, hh))
    return pl.pallas_call(
        functools.partial(_dil_prompt_kernel, group=group, dil=dil, win_steps=win // dil, nq=nq),
        grid=(batch, nj, DIL_HPG),
        in_specs=[pl.BlockSpec(memory_space=pltpu.SMEM), cur(OFF_DQ), cur(OFF_DK), prev(OFF_DK), cur(OFF_DV), prev(OFF_DV)],
        out_specs=[out_spec, out_spec],
        out_shape=[jax.ShapeDtypeStruct((batch * seq, DIL_W), F32)] * 2,
        compiler_params=_params(32), name=f"dilated_prompt_g{group}")(slopes, proj, proj, proj, proj, proj)


def _dil_sample_kernel(q_ref, kn_ref, vn_ref, kc_ref, vc_ref, o_ref, lse_ref, ok_ref, ov_ref,
                       *, group, dil, win_steps, ts):
    H = DIL_HPG
    buf = kc_ref.shape[0] // H
    P = q_ref.shape[0]
    t = lax.broadcasted_iota(I32, (P, buf), 0)
    r = lax.broadcasted_iota(I32, (P, buf), 1)
    diff = buf + t - r
    span = win_steps * dil
    ok = ((diff & (dil - 1)) == 0) & (diff <= span) & (diff >= 0)
    bias_steps = diff.astype(F32)
    t1 = lax.broadcasted_iota(I32, (P, 1), 0)
    scale = DIL_HD ** -0.5
    for hh in range(DIL_HPG):
        slope = _alibi_slope(group * DIL_HPG + hh)
        c = slice(hh * DIL_HD, (hh + 1) * DIL_HD)
        q = q_ref[:, c]
        head_rows = pl.ds(hh, buf, stride=H)
        s = lax.dot_general(q.astype(BF16), kc_ref[head_rows, :].astype(BF16), _NT, preferred_element_type=F32)
        s = jnp.where(ok, s * scale - slope * bias_steps, -jnp.inf)
        m = jnp.max(s, axis=-1, keepdims=True)
        s_new = []
        for rp in range(ts):
            d = t1 - rp
            ok_n = (d >= 0) & ((d & (dil - 1)) == 0) & (d <= span)
            new_row = slice(rp * H + hh, rp * H + hh + 1)
            sn = jnp.sum(q * kn_ref[new_row, :], axis=-1, keepdims=True) * scale - slope * d.astype(F32)
            sn = jnp.where(ok_n, sn, -jnp.inf)
            s_new.append(sn)
            m = jnp.maximum(m, sn)
        e = jnp.exp(s - m)
        e_new = [jnp.exp(sn - m) for sn in s_new]
        l = jnp.sum(e, axis=-1, keepdims=True)
        for en in e_new:
            l = l + en
        o = jnp.dot((e / l).astype(BF16), vc_ref[head_rows, :].astype(BF16), preferred_element_type=F32)
        for rp in range(ts):
            o = o + (e_new[rp] / l) * vn_ref[rp * H + hh:rp * H + hh + 1, :]
        o_ref[:, c] = o
        lse_ref[:, c] = jnp.broadcast_to(m + jnp.log(l), (P, DIL_HD))
    ok_ref[0:(buf - ts) * H, :] = kc_ref[ts * H:buf * H, :]
    ok_ref[(buf - ts) * H:buf * H, :] = kn_ref[...]
    ov_ref[0:(buf - ts) * H, :] = vc_ref[ts * H:buf * H, :]
    ov_ref[(buf - ts) * H:buf * H, :] = vn_ref[...]


def _dilated_sample(q8, kn, vn, cache_k, cache_v, layer, group, ts):
    win, dil = DIL_GROUPS[group]
    depth, ns, buf = cache_k.shape[:3]
    assert buf == win and buf % dil == 0, "cached window must hold exactly one full window"
    flat = lambda c: c.reshape(depth, ns, buf * DIL_HPG, DIL_HD)
    small = pl.BlockSpec((None, SAMPLE_PAD, DIL_W), lambda n: (n, 0, 0))
    new = pl.BlockSpec((None, ts * DIL_HPG, DIL_HD), lambda n: (n, 0, 0))
    big_in = pl.BlockSpec((None, None, buf * DIL_HPG, DIL_HD), lambda n: (layer, n, 0, 0))
    big_out = pl.BlockSpec((None, buf * DIL_HPG, DIL_HD), lambda n: (n, 0, 0))
    o, lse, k_out, v_out = pl.pallas_call(
        functools.partial(_dil_sample_kernel, group=group, dil=dil, win_steps=win // dil, ts=ts),
        grid=(ns,), in_specs=[small, new, new, big_in, big_in],
        out_specs=[small, small, big_out, big_out],
        out_shape=[jax.ShapeDtypeStruct((ns, SAMPLE_PAD, DIL_W), F32)] * 2
        + [jax.ShapeDtypeStruct((ns, buf * DIL_HPG, DIL_HD), F32)] * 2,
        compiler_params=_params(56), name=f"dilated_sample_g{group}")(q8, kn, vn, flat(cache_k), flat(cache_v))
    return o, lse, k_out.reshape(ns, buf, DIL_HPG, DIL_HD), v_out.reshape(ns, buf, DIL_HPG, DIL_HD)


def _combine_kernel(o1_ref, o2_ref, o3_ref, l1_ref, l2_ref, l3_ref, out_ref):
    l1, l2, l3 = l1_ref[...], l2_ref[...], l3_ref[...]
    m = jnp.maximum(jnp.maximum(l1, l2), l3)
    e1, e2, e3 = jnp.exp(l1 - m), jnp.exp(l2 - m), jnp.exp(l3 - m)
    z = e1 + e2 + e3
    out = (e1 / z) * o1_ref[...] + (e2 / z) * o2_ref[...] + (e3 / z) * o3_ref[...]
    out_ref[...] = out.astype(out_ref.dtype)


def _combine_groups(outs, lses):
    m, w = outs[0].shape
    tm = _pick(m, (1024, 512, 256, 128, 64, 8))
    spec = pl.BlockSpec((tm, w), lambda i: (i, 0))
    return pl.pallas_call(_combine_kernel, grid=(m // tm,), in_specs=[spec] * 6, out_specs=spec,
                          out_shape=jax.ShapeDtypeStruct((m, w), BF16),
                          compiler_params=_params(40), name="combine_groups")(*outs, *lses)


def _mem_attn_head(q, k, v):
    s = lax.dot_general(q.astype(BF16), k.astype(BF16), _NT, preferred_element_type=F32) * (MEM_HD ** -0.5)
    e = jnp.exp(s - jnp.max(s, axis=-1, keepdims=True))
    p = e / jnp.sum(e, axis=-1, keepdims=True)
    return jnp.dot(p.astype(BF16), v.astype(BF16), preferred_element_type=F32)


def _mem_attn_kernel(q_ref, k_ref, v_ref, o_ref):
    for h in range(MEM_HEADS):
        c = slice(h * MEM_HD, (h + 1) * MEM_HD)
        o_ref[:, c] = _mem_attn_head(q_ref[:, c], k_ref[:, c], v_ref[:, c]).astype(o_ref.dtype)


def _mem_attn_cached_kernel(q_ref, k_ref, v_ref, o_ref):
    chunks = MEM_HD // 128
    mem_len = k_ref.shape[0] // (MEM_HEADS * chunks)
    for h in range(MEM_HEADS):
        rows = [pl.ds(h * chunks + c, mem_len, stride=MEM_HEADS * chunks) for c in range(chunks)]
        cols = [slice(h * MEM_HD + c * 128, h * MEM_HD + (c + 1) * 128) for c in range(chunks)]
        s = sum(lax.dot_general(q_ref[:, cols[c]].astype(BF16), k_ref[rows[c], :].astype(BF16), _NT,
                                preferred_element_type=F32) for c in range(chunks)) * (MEM_HD ** -0.5)
        e = jnp.exp(s - jnp.max(s, axis=-1, keepdims=True))
        p = (e / jnp.sum(e, axis=-1, keepdims=True)).astype(BF16)
        for c in range(chunks):
            o_ref[:, cols[c]] = jnp.dot(p, v_ref[rows[c], :].astype(BF16),
                                        preferred_element_type=F32).astype(o_ref.dtype)


def _memory_attend_prompt(proj, mkv, batch, seq):
    mem_len = mkv.shape[1]
    tq = _pick(seq, (512, 256, 128))
    nq = seq // tq
    return pl.pallas_call(
        _mem_attn_kernel, grid=(batch, nq),
        in_specs=[pl.BlockSpec((tq, MEM_W), lambda b, i: (b * nq + i, OFF_MQ // MEM_W)),
                  pl.BlockSpec((None, mem_len, MEM_W), lambda b, i: (b, 0, 0)),
                  pl.BlockSpec((None, mem_len, MEM_W), lambda b, i: (b, 0, 1))],
        out_specs=pl.BlockSpec((tq, MEM_W), lambda b, i: (b * nq + i, 0)),
        out_shape=jax.ShapeDtypeStruct((batch * seq, MEM_W), BF16),
        compiler_params=_params(40), name="memory_attend_prompt")(proj, mkv, mkv)


def _memory_attend_sample(q8, mem_k, mem_v, layer):
    depth, ns, mem_len = mem_k.shape[:3]
    rows = mem_len * MEM_W // 128
    flat = lambda c: c.reshape(depth, ns, rows, 128)
    small = pl.BlockSpec((None, SAMPLE_PAD, MEM_W), lambda n: (n, 0, 0))
    big = pl.BlockSpec((None, None, rows, 128), lambda n: (layer, n, 0, 0))
    return pl.pallas_call(
        _mem_attn_cached_kernel, grid=(ns,), in_specs=[small, big, big], out_specs=small,
        out_shape=jax.ShapeDtypeStruct((ns, SAMPLE_PAD, MEM_W), F32),
        compiler_params=_params(24), name="memory_attend_sample")(q8, flat(mem_k), flat(mem_v))


def _mix_kernel(ar_ref, ad_ref, am_ref, wr_ref, wd_ref, wm_ref, gr_ref, gd_ref, gm_ref, o_ref):
    r = jnp.dot(ar_ref[...], wr_ref[...], preferred_element_type=F32)
    d = jnp.dot(ad_ref[...], wd_ref[...], preferred_element_type=F32)
    m = jnp.dot(am_ref[...], wm_ref[...], preferred_element_type=F32)
    mix = jax.nn.sigmoid(gr_ref[...]) * r + jax.nn.sigmoid(gd_ref[...]) * d + jax.nn.sigmoid(gm_ref[...]) * m
    o_ref[...] = mix.astype(o_ref.dtype)


def _branch_mix(a_r, a_d, a_m, w_r, w_d, w_m, proj, row0):
    t = a_r.shape[0]
    tm = _row_tile(t, row0, (1024, 832, 640, 512, 416, 320, 256, 128, 64, 8))
    blk0 = row0 // tm
    tn = 512
    nj = D_MODEL // tn
    lhs = lambda w: pl.BlockSpec((tm, w), lambda i, j: (i, 0))
    rhs = lambda w: pl.BlockSpec((w, tn), lambda i, j: (0, j))
    gate = lambda b: pl.BlockSpec((tm, tn), lambda i, j: (blk0 + i, OFF_GATE // tn + b * nj + j))
    return pl.pallas_call(
        _mix_kernel, grid=(t // tm, nj),
        in_specs=[lhs(RET_V), lhs(DIL_W), lhs(MEM_W), rhs(RET_V), rhs(DIL_W), rhs(MEM_W), gate(0), gate(1), gate(2)],
        out_specs=pl.BlockSpec((tm, tn), lambda i, j: (i, j)),
        out_shape=jax.ShapeDtypeStruct((t, D_MODEL), BF16),
        compiler_params=_params(56), name="branch_mix")(a_r, a_d, a_m, w_r, w_d, w_m, proj, proj, proj)


def _staircase(k):
    return [(a, k // (a + 1)) for a in range(k)]


def _peer_topk_kernel(q_ref, keys_ref, i_ref, j_ref, g_ref, s_scr, key_scr, cand_scr, pos_scr):
    tk = q_ref.shape[0]
    K = PEER_TOPK
    SUB = 8
    iota_k = lax.broadcasted_iota(I32, (K, tk), 0)

    def top_k_rows(score_refs, id_ref):
        def body(k, carry):
            sel = iota_k == k
            out = []
            for ref, (prev, top_s, top_i) in zip(score_refs, carry):
                best = best_id = None
                for v in range(ref.shape[0] // SUB):
                    rows = slice(v * SUB, (v + 1) * SUB)
                    ids = id_ref[rows, :]
                    sv = jnp.where(ids == prev, -jnp.inf, ref[rows, :])
                    ref[rows, :] = sv
                    if best is None:
                        best, best_id = sv, ids
                    else:
                        best_id = jnp.where(sv > best, ids, best_id)
                        best = jnp.maximum(best, sv)
                m = jnp.max(best, axis=0, keepdims=True)
                pos = jnp.min(jnp.where(best == m, best_id, jnp.iinfo(jnp.int32).max), axis=0, keepdims=True)
                out.append((pos, jnp.where(sel, m, top_s), jnp.where(sel, pos, top_i)))
            return tuple(out)

        init = tuple((jnp.full((1, tk), -1, I32), jnp.zeros((K, tk), F32), jnp.zeros((K, tk), I32))
                     for _ in score_refs)
        return [(top_s, top_i) for _, top_s, top_i in lax.fori_loop(0, K, body, init)]

    for c in range(2):
        qc = q_ref[:, c * PEER_NKEYS:(c + 1) * PEER_NKEYS].astype(BF16)
        s_scr[c] = lax.dot_general(keys_ref[0, c].astype(BF16), qc, _NT, preferred_element_type=F32)
    key_scr[...] = lax.broadcasted_iota(I32, key_scr.shape, 0)
    (s1, i1), (s2, i2) = top_k_rows([s_scr.at[0], s_scr.at[1]], key_scr)

    cand_scr[...] = jnp.full(cand_scr.shape, -jnp.inf, F32)
    pos_scr[...] = jnp.full(pos_scr.shape, K * K, I32)
    off = 0
    for a, nb in _staircase(K):
        cand_scr[off:off + nb, :] = s1[a:a + 1, :] + s2[0:nb, :]
        pos_scr[off:off + nb, :] = lax.broadcasted_iota(I32, (nb, tk), 0) + a * K
        off += nb
    (best_s, best_p), = top_k_rows([cand_scr], pos_scr)
    pa = jnp.right_shift(best_p, K.bit_length() - 1)
    pb = best_p - pa * K
    ei = jnp.zeros((K, tk), I32)
    ej = jnp.zeros((K, tk), I32)
    for a in range(K):
        ei = jnp.where(pa == a, i1[a:a + 1, :], ei)
        ej = jnp.where(pb == a, i2[a:a + 1, :], ej)
    e = jnp.exp(best_s - jnp.max(best_s, axis=0, keepdims=True))
    i_ref[...] = ei.astype(F32)
    j_ref[...] = ej.astype(F32)
    g_ref[...] = e / jnp.sum(e, axis=0, keepdims=True)


def _peer_route(q, sub_keys):
    t = q.shape[0]
    tk = _pick(t, (640, 512, 256, 128))
    n_cand = -(-sum(nb for _, nb in _staircase(PEER_TOPK)) // 8) * 8
    out_spec = pl.BlockSpec((PEER_TOPK, tk), lambda i, h: (h, i))
    return pl.pallas_call(
        _peer_topk_kernel, grid=(t // tk, PEER_HEADS),
        in_specs=[pl.BlockSpec((tk, PEER_DQ), lambda i, h: (i, h)),
                  pl.BlockSpec((1, 2, PEER_NKEYS, PEER_DQ // 2), lambda i, h: (h, 0, 0, 0))],
        out_specs=[out_spec] * 3,
        out_shape=[jax.ShapeDtypeStruct((PEER_HEADS * PEER_TOPK, t), F32)] * 3,
        scratch_shapes=[pltpu.VMEM((2, PEER_NKEYS, tk), F32), pltpu.VMEM((PEER_NKEYS, tk), I32),
                        pltpu.VMEM((n_cand, tk), F32), pltpu.VMEM((n_cand, tk), I32)],
        compiler_params=_params(24), name="peer_route")(q, sub_keys)


def _peer_wbuild_kernel(i_ref, j_ref, g_ref, w_ref, it_scr, jt_scr, gt_scr, wt_scr):
    tw = w_ref.shape[0]
    it_scr[...] = i_ref[...].T
    jt_scr[...] = j_ref[...].T
    gt_scr[...] = g_ref[...].T
    n = PEER_NKEYS
    iota_r = lax.broadcasted_iota(I32, (n, i_ref.shape[0]), 0).astype(F32)

    def body(t, carry):
        irow = it_scr[pl.ds(t, 1), :]
        jrow = jt_scr[pl.ds(t, 1), :]
        grow = gt_scr[pl.ds(t, 1), :]
        a_t = jnp.where(iota_r == irow, grow, 0.0).astype(BF16)
        b_t = jnp.where(iota_r == jrow, 1.0, 0.0).astype(BF16)
        w = lax.dot_general(a_t, b_t, _NT, preferred_element_type=F32)
        wt_scr[pl.ds(pl.multiple_of(t * W_ROW_PITCH, 8), n), :] = w
        return carry

    lax.fori_loop(0, tw, body, 0, unroll=64)
    for i in range(n):
        w_ref[:, i * n:(i + 1) * n] = wt_scr[pl.ds(i, tw, stride=W_ROW_PITCH), :].astype(BF16)


def _peer_wbuild(ei, ej, g):
    nsel, t = ei.shape
    tw = 128
    in_spec = pl.BlockSpec((nsel, tw), lambda i: (0, i))
    n = PEER_NKEYS
    return pl.pallas_call(
        _peer_wbuild_kernel, grid=(t // tw,), in_specs=[in_spec] * 3,
        out_specs=pl.BlockSpec((tw, n * n), lambda i: (i, 0)),
        out_shape=jax.ShapeDtypeStruct((t, n * n), BF16),
        scratch_shapes=[pltpu.VMEM((tw, nsel), F32)] * 3 + [pltpu.VMEM((tw * W_ROW_PITCH, n), F32)],
        compiler_params=_params(32), name="peer_wbuild")(ei, ej, g)


def _gelu_tanh(x):
    return x * (0.5 * (1.0 + jnp.tanh(math.sqrt(2.0 / math.pi) * (x + 0.044715 * (x * x * x)))))


def _peer_mlp_kernel(x_ref, u_ref, v_ref, w_ref, o_ref):
    e = pl.program_id(1)

    @pl.when(e == 0)
    def _():
        o_ref[...] = jnp.zeros_like(o_ref)

    half = u_ref.shape[0] // 2
    ps = []
    for c in range(2):
        rows = slice(c * half, (c + 1) * half)
        act = lax.dot_general(x_ref[...], u_ref[rows, :], _NT, preferred_element_type=F32)
        ps.append((w_ref[:, rows].astype(F32) * _gelu_tanh(act)).astype(BF16))
    p = jnp.concatenate(ps, axis=1)
    o_ref[...] = o_ref[...] + jnp.dot(p, v_ref[...], preferred_element_type=F32)


def _peer_mlp(n2, u, v, w):
    t, d = n2.shape
    ne = u.shape[0]
    tm = _pick(t, (640, 416, 320, 256, 128, 64, 8))
    eb = 512
    return pl.pallas_call(
        _peer_mlp_kernel, grid=(t // tm, ne // eb),
        in_specs=[pl.BlockSpec((tm, d), lambda i, e: (i, 0)),
                  pl.BlockSpec((eb, d), lambda i, e: (e, 0)),
                  pl.BlockSpec((eb, d), lambda i, e: (e, 0)),
                  pl.BlockSpec((tm, eb), lambda i, e: (i, e))],
        out_specs=pl.BlockSpec((tm, d), lambda i, e: (i, 0)),
        out_shape=jax.ShapeDtypeStruct((t, d), F32),
        compiler_params=_params(62), name="peer_mlp")(n2, u, v, w)


def _pad_sample_rows(a, ns, ts):
    a = a.reshape(ns, ts, a.shape[-1])
    return jnp.pad(a, ((0, 0), (0, SAMPLE_PAD - ts), (0, 0)))


def kernel(x_prompt, x_sample, mem_prompt, cache_ret_state, cache_win_k1, cache_win_v1, cache_win_k2, cache_win_v2, cache_win_k3, cache_win_v3, cache_mem_k, cache_mem_v, norm1_w, w_in, ret_gn_w, w_ret_o, w_dil_o, mem_norm_w, w_mem_kv, w_mem_o, w_out, norm2_w, w_peer_q, peer_sub_keys, peer_u, peer_v, final_norm_w):
    batch, seq, d = x_prompt.shape
    ns, ts, _ = x_sample.shape
    mem_len = mem_prompt.shape[1]
    depth = norm1_w.shape[0]
    assert d == D_MODEL and seq % RET_CHUNK == 0 and ts <= SAMPLE_PAD
    tp, tsamp = batch * seq, ns * ts
    win_k = (cache_win_k1, cache_win_k2, cache_win_k3)
    win_v = (cache_win_v1, cache_win_v2, cache_win_v3)

    slopes = jnp.asarray([_alibi_slope(h) for h in range(DIL_HEADS)], F32)
    x_p, x_s = x_prompt.reshape(tp, d), x_sample.reshape(tsamp, d)
    prompt_states, sample_states = [], []
    for l in range(depth):
        proj = _matmul(_rmsnorm_pair(x_p, x_s, norm1_w[l], BF16), w_in[l],
                       tm_cands=(1664, 832, 640, 512, 416, 320, 256, 128, 64, 8), name="input_projection")
        proj_s = proj[tp:]

        ar_p, ret_p = _retention_prompt(proj, ret_gn_w[l], batch, seq)
        ar_s, ret_s = _retention_sample(proj_s, ret_gn_w[l], cache_ret_state[l], ts)

        outs_p, lses_p, outs_s, lses_s, bufs_p, bufs_s = [], [], [], [], [], []
        for g, (win, dil) in enumerate(DIL_GROUPS):
            o, lse = _dilated_prompt(proj, slopes, g, batch, seq)
            outs_p.append(o)
            lses_p.append(lse)
            keep = min(win, seq)
            for off in (OFF_DK, OFF_DV):
                c0 = off + g * DIL_W
                tail = [proj[b * seq + seq - keep:(b + 1) * seq, c0:c0 + DIL_W] for b in range(batch)]
                bufs_p.append(jnp.stack(tail, axis=0).reshape(batch, keep, DIL_HPG, DIL_HD))
            q_s, k_s, v_s = (proj_s[:, off + g * DIL_W: off + (g + 1) * DIL_W] for off in (OFF_DQ, OFF_DK, OFF_DV))
            o8, lse8, k_new, v_new = _dilated_sample(
                _pad_sample_rows(q_s, ns, ts), k_s.reshape(ns, ts * DIL_HPG, DIL_HD), v_s.reshape(ns, ts * DIL_HPG, DIL_HD),
                win_k[g], win_v[g], l, g, ts)
            outs_s.append(o8[:, :ts].reshape(tsamp, DIL_W))
            lses_s.append(lse8[:, :ts].reshape(tsamp, DIL_W))
            bufs_s += [k_new, v_new]
        ad_p = _combine_groups(outs_p, lses_p)
        ad_s = _combine_groups(outs_s, lses_s)

        mem_n = _rmsnorm(mem_prompt.reshape(batch * mem_len, d), mem_norm_w[l], BF16)
        mkv = _matmul(mem_n, w_mem_kv[l].astype(BF16), name="memory_kv").reshape(batch, mem_len, 2 * MEM_W)
        am_p = _memory_attend_prompt(proj, mkv, batch, seq)
        mq8 = _pad_sample_rows(proj_s[:, OFF_MQ:OFF_MQ + MEM_W], ns, ts)
        am_s = _memory_attend_sample(mq8, cache_mem_k, cache_mem_v, l)
        am_s = am_s[:, :ts].reshape(tsamp, MEM_W).astype(BF16)

        branch_w = (w_ret_o[l].astype(BF16), w_dil_o[l].astype(BF16), w_mem_o[l].astype(BF16))
        w_out_b = w_out[l].astype(BF16)
        h_p = _matmul(_branch_mix(ar_p, ad_p, am_p, *branch_w, proj, 0), w_out_b, res=x_p, name="output_projection")
        h_s = _matmul(_branch_mix(ar_s, ad_s, am_s, *branch_w, proj, tp), w_out_b, res=x_s, name="output_projection")

        n2 = _rmsnorm_pair(h_p, h_s, norm2_w[l], BF16)
        pq = _matmul(n2, w_peer_q[l].astype(BF16), name="peer_query")
        ei, ej, gw = _peer_route(pq, peer_sub_keys[l])
        w_dense = _peer_wbuild(ei, ej, gw)
        peer = _peer_mlp(n2, peer_u[l].astype(BF16), peer_v[l].astype(BF16), w_dense)

        mk = mkv[:, :, :MEM_W].reshape(batch, mem_len, MEM_HEADS, MEM_HD)
        mv = mkv[:, :, MEM_W:].reshape(batch, mem_len, MEM_HEADS, MEM_HD)
        prompt_states.append((ret_p, *bufs_p, mk, mv))
        sample_states.append((ret_s, *bufs_s))
        if l + 1 < depth:
            x_p, x_s = _add(h_p, peer, 0), _add(h_s, peer, tp)

    y_prompt = _add_rmsnorm(h_p, peer, final_norm_w, 0).reshape(batch, seq, d)
    y_sample = _add_rmsnorm(h_s, peer, final_norm_w, tp).reshape(ns, ts, d)
    p_out = [jnp.stack(s, axis=0) for s in zip(*prompt_states)]
    s_out = [jnp.stack(s, axis=0) for s in zip(*sample_states)]
    return (y_prompt, y_sample, *p_out, *s_out)
```

```python
import functools
import math

import jax
import jax.numpy as jnp
from jax import lax
from jax.experimental import pallas as pl
from jax.experimental.pallas import tpu as pltpu

F32, BF16, I32 = jnp.float32, jnp.bfloat16, jnp.int32

D_MODEL = 4096
RET_HEADS, RET_DK, RET_DV, RET_CHUNK = 8, 128, 256, 128
DIL_GROUPS = ((128, 1), (512, 4), (2048, 16))
DIL_HPG, DIL_HD, DIL_BLOCK = 4, 128, 128
DIL_HEADS = DIL_HPG * len(DIL_GROUPS)
ALIBI_MAX_EXP = 8.0
MEM_HEADS, MEM_HD = 4, 384
PEER_HEADS, PEER_NKEYS, PEER_TOPK, PEER_DQ = 8, 128, 16, 256
NORM_EPS = 1e-6

RET_QK = RET_HEADS * RET_DK
RET_V = RET_HEADS * RET_DV
DIL_W = DIL_HPG * DIL_HD
DIL_ALL = DIL_HEADS * DIL_HD
MEM_W = MEM_HEADS * MEM_HD
OFF_RQ = 0
OFF_RK = OFF_RQ + RET_QK
OFF_RV = OFF_RK + RET_QK
OFF_RG = OFF_RV + RET_V
OFF_DQ = OFF_RG + RET_V
OFF_DK = OFF_DQ + DIL_ALL
OFF_DV = OFF_DK + DIL_ALL
OFF_MQ = OFF_DV + DIL_ALL
OFF_GATE = OFF_MQ + MEM_W
PROJ_W = OFF_GATE + 3 * D_MODEL

SAMPLE_PAD = 8
W_ROW_PITCH = PEER_NKEYS + 8
V7X_VMEM_BYTES = 64 * 1024 * 1024
MIB = 1024 * 1024

_NT = (((1,), (1,)), ((), ()))


def _pick(n, cands):
    for c in cands:
        if n % c == 0:
            return c
    raise ValueError(f"no tile in {cands} divides {n}")


def _params(vmem_mib):
    return pltpu.CompilerParams(vmem_limit_bytes=min(vmem_mib * MIB, V7X_VMEM_BYTES - 2 * MIB))


def _rmsnorm_kernel(x_ref, w_ref, o_ref):
    x = x_ref[...]
    ms = jnp.mean(x * x, axis=-1, keepdims=True)
    o_ref[...] = (x * lax.rsqrt(ms + NORM_EPS) * w_ref[...]).astype(o_ref.dtype)


def _rmsnorm(x, w, out_dtype):
    m, d = x.shape
    tm = _pick(m, (320, 256, 128, 64, 8))
    return pl.pallas_call(
        _rmsnorm_kernel, grid=(m // tm,),
        in_specs=[pl.BlockSpec((tm, d), lambda i: (i, 0)), pl.BlockSpec((1, d), lambda i: (0, 0))],
        out_specs=pl.BlockSpec((tm, d), lambda i: (i, 0)),
        out_shape=jax.ShapeDtypeStruct((m, d), out_dtype),
        compiler_params=_params(48), name="rmsnorm")(x, w.reshape(1, d))


def _add_rmsnorm_kernel(a_ref, b_ref, w_ref, o_ref):
    x = a_ref[...] + b_ref[...]
    ms = jnp.mean(x * x, axis=-1, keepdims=True)
    o_ref[...] = x * lax.rsqrt(ms + NORM_EPS) * w_ref[...]


def _rmsnorm_pair_kernel(a_ref, b_ref, w_ref, o_ref, *, na):
    i = pl.program_id(0)

    def norm(x):
        ms = jnp.mean(x * x, axis=-1, keepdims=True)
        return (x * lax.rsqrt(ms + NORM_EPS) * w_ref[...]).astype(o_ref.dtype)

    @pl.when(i < na)
    def _():
        o_ref[...] = norm(a_ref[...])

    @pl.when(i >= na)
    def _():
        o_ref[0:b_ref.shape[0], :] = norm(b_ref[...])


def _rmsnorm_pair(a, b, w, out_dtype):
    d = a.shape[1]
    tm = _pick(a.shape[0], (512, 256, 128, 64, 8))
    assert b.shape[0] <= tm and b.shape[0] % 8 == 0
    na = a.shape[0] // tm
    return pl.pallas_call(
        functools.partial(_rmsnorm_pair_kernel, na=na), grid=(na + 1,),
        in_specs=[pl.BlockSpec((tm, d), lambda i: (jnp.minimum(i, na - 1), 0)),
                  pl.BlockSpec(b.shape, lambda i: (0, 0)),
                  pl.BlockSpec((1, d), lambda i: (0, 0))],
        out_specs=pl.BlockSpec((tm, d), lambda i: (i, 0)),
        out_shape=jax.ShapeDtypeStruct((a.shape[0] + b.shape[0], d), out_dtype),
        compiler_params=_params(48), name="rmsnorm_stream")(a, b, w.reshape(1, d))


def _row_tile(rows, row0, cands):
    return _pick(math.gcd(row0, rows) if row0 else rows, cands)


def _add_rmsnorm(a, b, w, b_row0):
    rows, d = a.shape
    tm = _row_tile(rows, b_row0, (256, 128, 64, 8))
    blk0 = b_row0 // tm
    return pl.pallas_call(
        _add_rmsnorm_kernel, grid=(rows // tm,),
        in_specs=[pl.BlockSpec((tm, d), lambda i: (i, 0)), pl.BlockSpec((tm, d), lambda i: (blk0 + i, 0)),
                  pl.BlockSpec((1, d), lambda i: (0, 0))],
        out_specs=pl.BlockSpec((tm, d), lambda i: (i, 0)),
        out_shape=jax.ShapeDtypeStruct((rows, d), F32),
        compiler_params=_params(56), name="add_rmsnorm")(a, b, w.reshape(1, d))


def _add_kernel(a_ref, b_ref, o_ref):
    o_ref[...] = a_ref[...] + b_ref[...]


def _add(a, b, b_row0):
    rows, d = a.shape
    tm = _row_tile(rows, b_row0, (256, 128, 64, 8))
    blk0 = b_row0 // tm
    spec = pl.BlockSpec((tm, d), lambda i: (i, 0))
    return pl.pallas_call(_add_kernel, grid=(rows // tm,),
                          in_specs=[spec, pl.BlockSpec((tm, d), lambda i: (blk0 + i, 0))], out_specs=spec,
                          out_shape=jax.ShapeDtypeStruct((rows, d), F32),
                          compiler_params=_params(56), name="residual_add")(a, b)


def _mm_kernel(a_ref, b_ref, o_ref):
    o_ref[...] = jnp.dot(a_ref[...], b_ref[...].astype(a_ref.dtype), preferred_element_type=F32).astype(o_ref.dtype)


def _mm_res_kernel(a_ref, b_ref, r_ref, o_ref):
    o_ref[...] = r_ref[...] + jnp.dot(a_ref[...], b_ref[...], preferred_element_type=F32)


def _matmul(a, b, res=None, tm_cands=(1024, 832, 640, 512, 416, 320, 256, 128, 64, 8), tn=512, name="matmul"):
    m, k = a.shape
    n = b.shape[1]
    tm = _pick(m, tm_cands)
    tn = _pick(n, (tn, 256, 128))
    in_specs = [pl.BlockSpec((tm, k), lambda i, j: (i, 0)), pl.BlockSpec((k, tn), lambda i, j: (0, j))]
    args = [a, b]
    body = _mm_kernel
    if res is not None:
        in_specs.append(pl.BlockSpec((tm, tn), lambda i, j: (i, j)))
        args.append(res)
        body = _mm_res_kernel
    return pl.pallas_call(
        body, grid=(m // tm, n // tn), in_specs=in_specs,
        out_specs=pl.BlockSpec((tm, tn), lambda i, j: (i, j)),
        out_shape=jax.ShapeDtypeStruct((m, n), F32),
        compiler_params=_params(56), name=name)(*args)


def _ret_log_decay(h):
    return math.log(1.0 - 2.0 ** (-5.0 - h))


def _group_norm_gate(o, gnw, rg):
    mu = jnp.mean(o, axis=-1, keepdims=True)
    d = o - mu
    var = jnp.mean(d * d, axis=-1, keepdims=True)
    y = d * lax.rsqrt(var + NORM_EPS) * gnw
    return (y * (rg * jax.nn.sigmoid(rg))).astype(BF16)


def _ret_prompt_kernel(q_ref, k_ref, v_ref, rg_ref, gnw_ref, ar_ref, st_ref, state_scr):
    c = pl.program_id(1)
    L = RET_CHUNK

    @pl.when(c == 0)
    def _():
        state_scr[...] = jnp.zeros_like(state_scr)

    diff = (lax.broadcasted_iota(I32, (L, L), 0) - lax.broadcasted_iota(I32, (L, L), 1)).astype(F32)
    row_k = lax.broadcasted_iota(I32, (L, RET_DK), 0).astype(F32)
    row_v = lax.broadcasted_iota(I32, (L, RET_DV), 0).astype(F32)
    def first_matmuls(h):
        lg = _ret_log_decay(h)
        ck = slice(h * RET_DK, (h + 1) * RET_DK)
        q = q_ref[:, ck].astype(BF16)
        k = k_ref[:, ck] * (RET_DK ** -0.5)
        v = v_ref[:, h * RET_DV:(h + 1) * RET_DV].astype(BF16)
        s = lax.dot_general(q, k.astype(BF16), _NT, preferred_element_type=F32)
        st = state_scr[h]
        o_cross = jnp.dot(q, st.astype(BF16), preferred_element_type=F32)
        kd = (k * jnp.exp(lg * (L - 1.0 - row_k))).T.astype(BF16)
        state_scr[h] = st * math.exp(lg * L) + jnp.dot(kd, v, preferred_element_type=F32)
        return s, o_cross, v

    def decay_scores(h, s, o_cross, v):
        lg = _ret_log_decay(h)
        decay = jnp.where(diff >= 0, jnp.exp(lg * jnp.maximum(diff, 0.0)), 0.0)
        return (s * decay).astype(BF16), o_cross * jnp.exp(lg * (row_v + 1.0)), v

    def finish(h, inner, o_cross, v):
        cv = slice(h * RET_DV, (h + 1) * RET_DV)
        o = jnp.dot(inner, v, preferred_element_type=F32) + o_cross
        ar_ref[:, cv] = _group_norm_gate(o, gnw_ref[:, cv], rg_ref[:, cv])

    stage1, stage2 = {}, {}
    for step in range(RET_HEADS + 2):
        if step < RET_HEADS:
            stage1[step] = first_matmuls(step)
        if 0 <= step - 1 < RET_HEADS:
            stage2[step - 1] = decay_scores(step - 1, *stage1.pop(step - 1))
        if step - 2 >= 0:
            finish(step - 2, *stage2.pop(step - 2))

    @pl.when(c == pl.num_programs(1) - 1)
    def _():
        st_ref[0] = state_scr[...]


def _retention_prompt(proj, gnw, batch, seq):
    nc = seq // RET_CHUNK
    row = lambda n, c: n * nc + c
    return pl.pallas_call(
        _ret_prompt_kernel, grid=(batch, nc),
        in_specs=[pl.BlockSpec((RET_CHUNK, RET_QK), lambda n, c: (row(n, c), OFF_RQ // RET_QK)),
                  pl.BlockSpec((RET_CHUNK, RET_QK), lambda n, c: (row(n, c), OFF_RK // RET_QK)),
                  pl.BlockSpec((RET_CHUNK, RET_V), lambda n, c: (row(n, c), OFF_RV // RET_V)),
                  pl.BlockSpec((RET_CHUNK, RET_V), lambda n, c: (row(n, c), OFF_RG // RET_V)),
                  pl.BlockSpec((1, RET_V), lambda n, c: (0, 0))],
        out_specs=[pl.BlockSpec((RET_CHUNK, RET_V), lambda n, c: (row(n, c), 0)),
                   pl.BlockSpec((1, RET_HEADS, RET_DK, RET_DV), lambda n, c: (n, 0, 0, 0))],
        out_shape=[jax.ShapeDtypeStruct((batch * seq, RET_V), BF16),
                   jax.ShapeDtypeStruct((batch, RET_HEADS, RET_DK, RET_DV), F32)],
        scratch_shapes=[pltpu.VMEM((RET_HEADS, RET_DK, RET_DV), F32)],
        compiler_params=_params(32), name="retention_prompt")(proj, proj, proj, proj, gnw.reshape(1, RET_V))


def _ret_sample_kernel(q_ref, k_ref, v_ref, rg_ref, gnw_ref, st_in_ref, ar_ref, st_out_ref, o_scr, *, ts, ns_blk):
    i = pl.program_id(0)
    R = q_ref.shape[0]

    def sample_and_pos(shape, axis):
        r = lax.broadcasted_iota(I32, shape, axis).astype(F32)
        sid = jnp.floor((r + 0.5) * (1.0 / ts))
        return sid.astype(I32), r - sid * ts

    sid_r, t_r = sample_and_pos((R, R), 0)
    sid_c, t_c = sample_and_pos((R, R), 1)
    sid_k, t_k = sample_and_pos((R, RET_DK), 0)
    _, t_v = sample_and_pos((R, RET_DV), 0)
    for h in range(RET_HEADS):
        lg = _ret_log_decay(h)
        ck = slice(h * RET_DK, (h + 1) * RET_DK)
        cv = slice(h * RET_DV, (h + 1) * RET_DV)
        q = q_ref[:, ck]
        k = k_ref[:, ck] * (RET_DK ** -0.5)
        v = v_ref[:, cv].astype(BF16)

        @pl.when(i == 0)
        def _():
            s = lax.dot_general(q.astype(BF16), k.astype(BF16), _NT, preferred_element_type=F32)
            dt = t_r - t_c
            decay = jnp.where((sid_r == sid_c) & (dt >= 0), jnp.exp(lg * jnp.maximum(dt, 0.0)), 0.0)
            o_scr[:, cv] = jnp.dot((s * decay).astype(BF16), v, preferred_element_type=F32)

        q_dec = jnp.exp(lg * (t_v + 1.0))
        kd = k * jnp.exp(lg * (ts - 1.0 - t_k))
        for j in range(ns_blk):
            mine = sid_k == i * ns_blk + j
            st = st_in_ref[j, h]
            qm = jnp.where(mine, q, 0.0).astype(BF16)
            o_scr[:, cv] += jnp.dot(qm, st.astype(BF16), preferred_element_type=F32) * q_dec
            kdm = jnp.where(mine, kd, 0.0).T.astype(BF16)
            st_out_ref[j, h] = st * math.exp(lg * ts) + jnp.dot(kdm, v, preferred_element_type=F32)

    @pl.when(i == pl.num_programs(0) - 1)
    def _():
        for h in range(RET_HEADS):
            cv = slice(h * RET_DV, (h + 1) * RET_DV)
            ar_ref[:, cv] = _group_norm_gate(o_scr[:, cv], gnw_ref[:, cv], rg_ref[:, cv])


def _retention_sample(proj_s, gnw, state, ts):
    rows = proj_s.shape[0]
    ns = state.shape[0]
    ns_blk = _pick(ns, (4, 2, 1))
    return pl.pallas_call(
        functools.partial(_ret_sample_kernel, ts=ts, ns_blk=ns_blk), grid=(ns // ns_blk,),
        in_specs=[pl.BlockSpec((rows, RET_QK), lambda i: (0, OFF_RQ // RET_QK)),
                  pl.BlockSpec((rows, RET_QK), lambda i: (0, OFF_RK // RET_QK)),
                  pl.BlockSpec((rows, RET_V), lambda i: (0, OFF_RV // RET_V)),
                  pl.BlockSpec((rows, RET_V), lambda i: (0, OFF_RG // RET_V)),
                  pl.BlockSpec((1, RET_V), lambda i: (0, 0)),
                  pl.BlockSpec((ns_blk, RET_HEADS, RET_DK, RET_DV), lambda i: (i, 0, 0, 0))],
        out_specs=[pl.BlockSpec((rows, RET_V), lambda i: (0, 0)),
                   pl.BlockSpec((ns_blk, RET_HEADS, RET_DK, RET_DV), lambda i: (i, 0, 0, 0))],
        out_shape=[jax.ShapeDtypeStruct((rows, RET_V), BF16),
                   jax.ShapeDtypeStruct(state.shape, F32)],
        scratch_shapes=[pltpu.VMEM((rows, RET_V), F32)],
        compiler_params=_params(40), name="retention_sample")(
            proj_s, proj_s, proj_s, proj_s, gnw.reshape(1, RET_V), state)


def _alibi_slope(head):
    return 2.0 ** (-ALIBI_MAX_EXP * (head + 1.0) / DIL_HEADS)


def _dil_prompt_kernel(slope_ref, q_ref, k_ref, kp_ref, v_ref, vp_ref, o_ref, lse_ref, *, group, dil, win_steps, nq):
    j = pl.program_id(1)
    hh = pl.program_id(2)
    slope = slope_ref[group * DIL_HPG + hh]
    B = DIL_BLOCK
    qi = lax.broadcasted_iota(I32, (B, B), 0)
    kj = lax.broadcasted_iota(I32, (B, B), 1)
    steps_prev = qi + B - kj
    steps_cur = qi - kj
    in_prev = steps_prev <= win_steps
    in_prev_first = in_prev & (jnp.full((B, B), j, I32) > 0)
    in_cur = (steps_cur >= 0) & (steps_cur <= win_steps)
    bias_prev = slope * (steps_prev * dil).astype(F32)
    bias_cur = slope * (steps_cur * dil).astype(F32)
    scale = DIL_HD ** -0.5
    tiles = [(qb, r) for qb in range(nq) for r in range(dil)]

    def rows_of(qb, r):
        return pl.ds(qb * B * dil + r, B, stride=dil)

    def prev_of(ref, first_ref, qb, r):
        return first_ref[rows_of(0, r), :] if qb == 0 else ref[rows_of(qb - 1, r), :]

    def scores(qb, r):
        q = q_ref[rows_of(qb, r), :].astype(BF16)
        sp = lax.dot_general(q, prev_of(k_ref, kp_ref, qb, r).astype(BF16), _NT, preferred_element_type=F32)
        sc = lax.dot_general(q, k_ref[rows_of(qb, r), :].astype(BF16), _NT, preferred_element_type=F32)
        return sp, sc

    def softmax(qb, sp, sc):
        sp = jnp.where(in_prev_first if qb == 0 else in_prev, sp * scale - bias_prev, -jnp.inf)
        sc = jnp.where(in_cur, sc * scale - bias_cur, -jnp.inf)
        m = jnp.maximum(jnp.max(sp, axis=-1, keepdims=True), jnp.max(sc, axis=-1, keepdims=True))
        ep = jnp.exp(sp - m)
        ec = jnp.exp(sc - m)
        l = jnp.sum(ep, axis=-1, keepdims=True) + jnp.sum(ec, axis=-1, keepdims=True)
        return (ep / l).astype(BF16), (ec / l).astype(BF16), m + jnp.log(l)

    def finish(qb, r, pp, pc, lse):
        o = jnp.dot(pp, prev_of(v_ref, vp_ref, qb, r).astype(BF16), preferred_element_type=F32)
        o = o + jnp.dot(pc, v_ref[rows_of(qb, r), :].astype(BF16), preferred_element_type=F32)
        o_ref[rows_of(qb, r), :] = o
        lse_ref[rows_of(qb, r), :] = jnp.broadcast_to(lse, (B, DIL_HD))

    s_ready, p_ready = {}, {}
    lag_softmax, lag_finish = 2, 4
    for step in range(len(tiles) + lag_finish):
        if step < len(tiles):
            s_ready[step] = scores(*tiles[step])
        if 0 <= step - lag_softmax < len(tiles):
            p_ready[step - lag_softmax] = softmax(tiles[step - lag_softmax][0], *s_ready.pop(step - lag_softmax))
        if 0 <= step - lag_finish < len(tiles):
            finish(*tiles[step - lag_finish], *p_ready.pop(step - lag_finish))


def _dilated_prompt(proj, slopes, group, batch, seq):
    win, dil = DIL_GROUPS[group]
    span = DIL_BLOCK * dil
    assert seq % span == 0
    nq = _pick(seq // span, [n for n in (8, 4, 2, 1) if n * dil <= 16])
    nj = seq // (span * nq)
    col = lambda off: off // DIL_HD + group * DIL_HPG

    def cur(off):
        return pl.BlockSpec((span * nq, DIL_HD), lambda b, j, hh: (b * nj + j, col(off) + hh))

    def prev(off):
        return pl.BlockSpec((span, DIL_HD), lambda b, j, hh: (jnp.maximum((b * nj + j) * nq - 1, 0), col(off) + hh))

    out_spec = pl.BlockSpec((span * nq, DIL_HD), lambda b, j, hh: (b * nj + j, hh))
    return pl.pallas_call(
        functools.partial(_dil_prompt_kernel, group=group, dil=dil, win_steps=win // dil, nq=nq),
        grid=(batch, nj, DIL_HPG),
        in_specs=[pl.BlockSpec(memory_space=pltpu.SMEM), cur(OFF_DQ), cur(OFF_DK), prev(OFF_DK), cur(OFF_DV), prev(OFF_DV)],
        out_specs=[out_spec, out_spec],
        out_shape=[jax.ShapeDtypeStruct((batch * seq, DIL_W), F32)] * 2,
        compiler_params=_params(32), name=f"dilated_prompt_g{group}")(slopes, proj, proj, proj, proj, proj)


def _dil_sample_kernel(q_ref, kn_ref, vn_ref, kc_ref, vc_ref, o_ref, lse_ref, ok_ref, ov_ref,
                       *, group, dil, win_steps, ts):
    H = DIL_HPG
    buf = kc_ref.shape[0] // H
    P = q_ref.shape[0]
    t = lax.broadcasted_iota(I32, (P, buf), 0)
    r = lax.broadcasted_iota(I32, (P, buf), 1)
    diff = buf + t - r
    span = win_steps * dil
    ok = ((diff & (dil - 1)) == 0) & (diff <= span) & (diff >= 0)
    bias_steps = diff.astype(F32)
    t1 = lax.broadcasted_iota(I32, (P, 1), 0)
    scale = DIL_HD ** -0.5
    for hh in range(DIL_HPG):
        slope = _alibi_slope(group * DIL_HPG + hh)
        c = slice(hh * DIL_HD, (hh + 1) * DIL_HD)
        q = q_ref[:, c]
        head_rows = pl.ds(hh, buf, stride=H)
        s = lax.dot_general(q.astype(BF16), kc_ref[head_rows, :].astype(BF16), _NT, preferred_element_type=F32)
        s = jnp.where(ok, s * scale - slope * bias_steps, -jnp.inf)
        m = jnp.max(s, axis=-1, keepdims=True)
        s_new = []
        for rp in range(ts):
            d = t1 - rp
            ok_n = (d >= 0) & ((d & (dil - 1)) == 0) & (d <= span)
            new_row = slice(rp * H + hh, rp * H + hh + 1)
            sn = jnp.sum(q * kn_ref[new_row, :], axis=-1, keepdims=True) * scale - slope * d.astype(F32)
            sn = jnp.where(ok_n, sn, -jnp.inf)
            s_new.append(sn)
            m = jnp.maximum(m, sn)
        e = jnp.exp(s - m)
        e_new = [jnp.exp(sn - m) for sn in s_new]
        l = jnp.sum(e, axis=-1, keepdims=True)
        for en in e_new:
            l = l + en
        o = jnp.dot((e / l).astype(BF16), vc_ref[head_rows, :].astype(BF16), preferred_element_type=F32)
        for rp in range(ts):
            o = o + (e_new[rp] / l) * vn_ref[rp * H + hh:rp * H + hh + 1, :]
        o_ref[:, c] = o
        lse_ref[:, c] = jnp.broadcast_to(m + jnp.log(l), (P, DIL_HD))
    ok_ref[0:(buf - ts) * H, :] = kc_ref[ts * H:buf * H, :]
    ok_ref[(buf - ts) * H:buf * H, :] = kn_ref[...]
    ov_ref[0:(buf - ts) * H, :] = vc_ref[ts * H:buf * H, :]
    ov_ref[(buf - ts) * H:buf * H, :] = vn_ref[...]


def _dilated_sample(q8, kn, vn, cache_k, cache_v, layer, group, ts):
    win, dil = DIL_GROUPS[group]
    depth, ns, buf = cache_k.shape[:3]
    assert buf == win and buf % dil == 0, "cached window must hold exactly one full window"
    flat = lambda c: c.reshape(depth, ns, buf * DIL_HPG, DIL_HD)
    small = pl.BlockSpec((None, SAMPLE_PAD, DIL_W), lambda n: (n, 0, 0))
    new = pl.BlockSpec((None, ts * DIL_HPG, DIL_HD), lambda n: (n, 0, 0))
    big_in = pl.BlockSpec((None, None, buf * DIL_HPG, DIL_HD), lambda n: (layer, n, 0, 0))
    big_out = pl.BlockSpec((None, buf * DIL_HPG, DIL_HD), lambda n: (n, 0, 0))
    o, lse, k_out, v_out = pl.pallas_call(
        functools.partial(_dil_sample_kernel, group=group, dil=dil, win_steps=win // dil, ts=ts),
        grid=(ns,), in_specs=[small, new, new, big_in, big_in],
        out_specs=[small, small, big_out, big_out],
        out_shape=[jax.ShapeDtypeStruct((ns, SAMPLE_PAD, DIL_W), F32)] * 2
        + [jax.ShapeDtypeStruct((ns, buf * DIL_HPG, DIL_HD), F32)] * 2,
        compiler_params=_params(56), name=f"dilated_sample_g{group}")(q8, kn, vn, flat(cache_k), flat(cache_v))
    return o, lse, k_out.reshape(ns, buf, DIL_HPG, DIL_HD), v_out.reshape(ns, buf, DIL_HPG, DIL_HD)


def _combine_kernel(o1_ref, o2_ref, o3_ref, l1_ref, l2_ref, l3_ref, out_ref):
    l1, l2, l3 = l1_ref[...], l2_ref[...], l3_ref[...]
    m = jnp.maximum(jnp.maximum(l1, l2), l3)
    e1, e2, e3 = jnp.exp(l1 - m), jnp.exp(l2 - m), jnp.exp(l3 - m)
    z = e1 + e2 + e3
    out = (e1 / z) * o1_ref[...] + (e2 / z) * o2_ref[...] + (e3 / z) * o3_ref[...]
    out_ref[...] = out.astype(out_ref.dtype)


def _combine_groups(outs, lses):
    m, w = outs[0].shape
    tm = _pick(m, (1024, 512, 256, 128, 64, 8))
    spec = pl.BlockSpec((tm, w), lambda i: (i, 0))
    return pl.pallas_call(_combine_kernel, grid=(m // tm,), in_specs=[spec] * 6, out_specs=spec,
                          out_shape=jax.ShapeDtypeStruct((m, w), BF16),
                          compiler_params=_params(40), name="combine_groups")(*outs, *lses)


def _mem_attn_head(q, k, v):
    s = lax.dot_general(q.astype(BF16), k.astype(BF16), _NT, preferred_element_type=F32) * (MEM_HD ** -0.5)
    e = jnp.exp(s - jnp.max(s, axis=-1, keepdims=True))
    p = e / jnp.sum(e, axis=-1, keepdims=True)
    return jnp.dot(p.astype(BF16), v.astype(BF16), preferred_element_type=F32)


def _mem_attn_kernel(q_ref, k_ref, v_ref, o_ref):
    for h in range(MEM_HEADS):
        c = slice(h * MEM_HD, (h + 1) * MEM_HD)
        o_ref[:, c] = _mem_attn_head(q_ref[:, c], k_ref[:, c], v_ref[:, c]).astype(o_ref.dtype)


def _mem_attn_cached_kernel(q_ref, k_ref, v_ref, o_ref):
    chunks = MEM_HD // 128
    mem_len = k_ref.shape[0] // (MEM_HEADS * chunks)
    for h in range(MEM_HEADS):
        rows = [pl.ds(h * chunks + c, mem_len, stride=MEM_HEADS * chunks) for c in range(chunks)]
        cols = [slice(h * MEM_HD + c * 128, h * MEM_HD + (c + 1) * 128) for c in range(chunks)]
        s = sum(lax.dot_general(q_ref[:, cols[c]].astype(BF16), k_ref[rows[c], :].astype(BF16), _NT,
                                preferred_element_type=F32) for c in range(chunks)) * (MEM_HD ** -0.5)
        e = jnp.exp(s - jnp.max(s, axis=-1, keepdims=True))
        p = (e / jnp.sum(e, axis=-1, keepdims=True)).astype(BF16)
        for c in range(chunks):
            o_ref[:, cols[c]] = jnp.dot(p, v_ref[rows[c], :].astype(BF16),
                                        preferred_element_type=F32).astype(o_ref.dtype)


def _memory_attend_prompt(proj, mkv, batch, seq):
    mem_len = mkv.shape[1]
    tq = _pick(seq, (512, 256, 128))
    nq = seq // tq
    return pl.pallas_call(
        _mem_attn_kernel, grid=(batch, nq),
        in_specs=[pl.BlockSpec((tq, MEM_W), lambda b, i: (b * nq + i, OFF_MQ // MEM_W)),
                  pl.BlockSpec((None, mem_len, MEM_W), lambda b, i: (b, 0, 0)),
                  pl.BlockSpec((None, mem_len, MEM_W), lambda b, i: (b, 0, 1))],
        out_specs=pl.BlockSpec((tq, MEM_W), lambda b, i: (b * nq + i, 0)),
        out_shape=jax.ShapeDtypeStruct((batch * seq, MEM_W), BF16),
        compiler_params=_params(40), name="memory_attend_prompt")(proj, mkv, mkv)


def _memory_attend_sample(q8, mem_k, mem_v, layer):
    depth, ns, mem_len = mem_k.shape[:3]
    rows = mem_len * MEM_W // 128
    flat = lambda c: c.reshape(depth, ns, rows, 128)
    small = pl.BlockSpec((None, SAMPLE_PAD, MEM_W), lambda n: (n, 0, 0))
    big = pl.BlockSpec((None, None, rows, 128), lambda n: (layer, n, 0, 0))
    return pl.pallas_call(
        _mem_attn_cached_kernel, grid=(ns,), in_specs=[small, big, big], out_specs=small,
        out_shape=jax.ShapeDtypeStruct((ns, SAMPLE_PAD, MEM_W), F32),
        compiler_params=_params(24), name="memory_attend_sample")(q8, flat(mem_k), flat(mem_v))


def _mix_kernel(ar_ref, ad_ref, am_ref, wr_ref, wd_ref, wm_ref, gr_ref, gd_ref, gm_ref, o_ref):
    r = jnp.dot(ar_ref[...], wr_ref[...], preferred_element_type=F32)
    d = jnp.dot(ad_ref[...], wd_ref[...], preferred_element_type=F32)
    m = jnp.dot(am_ref[...], wm_ref[...], preferred_element_type=F32)
    mix = jax.nn.sigmoid(gr_ref[...]) * r + jax.nn.sigmoid(gd_ref[...]) * d + jax.nn.sigmoid(gm_ref[...]) * m
    o_ref[...] = mix.astype(o_ref.dtype)


def _branch_mix(a_r, a_d, a_m, w_r, w_d, w_m, proj, row0):
    t = a_r.shape[0]
    tm = _row_tile(t, row0, (1024, 832, 640, 512, 416, 320, 256, 128, 64, 8))
    blk0 = row0 // tm
    tn = 512
    nj = D_MODEL // tn
    lhs = lambda w: pl.BlockSpec((tm, w), lambda i, j: (i, 0))
    rhs = lambda w: pl.BlockSpec((w, tn), lambda i, j: (0, j))
    gate = lambda b: pl.BlockSpec((tm, tn), lambda i, j: (blk0 + i, OFF_GATE // tn + b * nj + j))
    return pl.pallas_call(
        _mix_kernel, grid=(t // tm, nj),
        in_specs=[lhs(RET_V), lhs(DIL_W), lhs(MEM_W), rhs(RET_V), rhs(DIL_W), rhs(MEM_W), gate(0), gate(1), gate(2)],
        out_specs=pl.BlockSpec((tm, tn), lambda i, j: (i, j)),
        out_shape=jax.ShapeDtypeStruct((t, D_MODEL), BF16),
        compiler_params=_params(56), name="branch_mix")(a_r, a_d, a_m, w_r, w_d, w_m, proj, proj, proj)


def _staircase(k):
    return [(a, k // (a + 1)) for a in range(k)]


def _peer_topk_kernel(q_ref, keys_ref, i_ref, j_ref, g_ref, s_scr, key_scr, cand_scr, pos_scr):
    tk = q_ref.shape[0]
    K = PEER_TOPK
    SUB = 8
    iota_k = lax.broadcasted_iota(I32, (K, tk), 0)

    def top_k_rows(score_refs, id_ref):
        def body(k, carry):
            sel = iota_k == k
            out = []
            for ref, (prev, top_s, top_i) in zip(score_refs, carry):
                best = best_id = None
                for v in range(ref.shape[0] // SUB):
                    rows = slice(v * SUB, (v + 1) * SUB)
                    ids = id_ref[rows, :]
                    sv = jnp.where(ids == prev, -jnp.inf, ref[rows, :])
                    ref[rows, :] = sv
                    if best is None:
                        best, best_id = sv, ids
                    else:
                        best_id = jnp.where(sv > best, ids, best_id)
                        best = jnp.maximum(best, sv)
                m = jnp.max(best, axis=0, keepdims=True)
                pos = jnp.min(jnp.where(best == m, best_id, jnp.iinfo(jnp.int32).max), axis=0, keepdims=True)
                out.append((pos, jnp.where(sel, m, top_s), jnp.where(sel, pos, top_i)))
            return tuple(out)

        init = tuple((jnp.full((1, tk), -1, I32), jnp.zeros((K, tk), F32), jnp.zeros((K, tk), I32))
                     for _ in score_refs)
        return [(top_s, top_i) for _, top_s, top_i in lax.fori_loop(0, K, body, init)]

    for c in range(2):
        qc = q_ref[:, c * PEER_NKEYS:(c + 1) * PEER_NKEYS].astype(BF16)
        s_scr[c] = lax.dot_general(keys_ref[0, c].astype(BF16), qc, _NT, preferred_element_type=F32)
    key_scr[...] = lax.broadcasted_iota(I32, key_scr.shape, 0)
    (s1, i1), (s2, i2) = top_k_rows([s_scr.at[0], s_scr.at[1]], key_scr)

    cand_scr[...] = jnp.full(cand_scr.shape, -jnp.inf, F32)
    pos_scr[...] = jnp.full(pos_scr.shape, K * K, I32)
    off = 0
    for a, nb in _staircase(K):
        cand_scr[off:off + nb, :] = s1[a:a + 1, :] + s2[0:nb, :]
        pos_scr[off:off + nb, :] = lax.broadcasted_iota(I32, (nb, tk), 0) + a * K
        off += nb
    (best_s, best_p), = top_k_rows([cand_scr], pos_scr)
    pa = jnp.right_shift(best_p, K.bit_length() - 1)
    pb = best_p - pa * K
    ei = jnp.zeros((K, tk), I32)
    ej = jnp.zeros((K, tk), I32)
    for a in range(K):
        ei = jnp.where(pa == a, i1[a:a + 1, :], ei)
        ej = jnp.where(pb == a, i2[a:a + 1, :], ej)
    e = jnp.exp(best_s - jnp.max(best_s, axis=0, keepdims=True))
    i_ref[...] = ei.astype(F32)
    j_ref[...] = ej.astype(F32)
    g_ref[...] = e / jnp.sum(e, axis=0, keepdims=True)


def _peer_route(q, sub_keys):
    t = q.shape[0]
    tk = _pick(t, (640, 512, 256, 128))
    n_cand = -(-sum(nb for _, nb in _staircase(PEER_TOPK)) // 8) * 8
    out_spec = pl.BlockSpec((PEER_TOPK, tk), lambda i, h: (h, i))
    return pl.pallas_call(
        _peer_topk_kernel, grid=(t // tk, PEER_HEADS),
        in_specs=[pl.BlockSpec((tk, PEER_DQ), lambda i, h: (i, h)),
                  pl.BlockSpec((1, 2, PEER_NKEYS, PEER_DQ // 2), lambda i, h: (h, 0, 0, 0))],
        out_specs=[out_spec] * 3,
        out_shape=[jax.ShapeDtypeStruct((PEER_HEADS * PEER_TOPK, t), F32)] * 3,
        scratch_shapes=[pltpu.VMEM((2, PEER_NKEYS, tk), F32), pltpu.VMEM((PEER_NKEYS, tk), I32),
                        pltpu.VMEM((n_cand, tk), F32), pltpu.VMEM((n_cand, tk), I32)],
        compiler_params=_params(24), name="peer_route")(q, sub_keys)


def _peer_wbuild_kernel(i_ref, j_ref, g_ref, w_ref, it_scr, jt_scr, gt_scr, wt_scr):
    tw = w_ref.shape[0]
    it_scr[...] = i_ref[...].T
    jt_scr[...] = j_ref[...].T
    gt_scr[...] = g_ref[...].T
    n = PEER_NKEYS
    iota_r = lax.broadcasted_iota(I32, (n, i_ref.shape[0]), 0).astype(F32)

    def body(t, carry):
        irow = it_scr[pl.ds(t, 1), :]
        jrow = jt_scr[pl.ds(t, 1), :]
        grow = gt_scr[pl.ds(t, 1), :]
        a_t = jnp.where(iota_r == irow, grow, 0.0).astype(BF16)
        b_t = jnp.where(iota_r == jrow, 1.0, 0.0).astype(BF16)
        w = lax.dot_general(a_t, b_t, _NT, preferred_element_type=F32)
        wt_scr[pl.ds(pl.multiple_of(t * W_ROW_PITCH, 8), n), :] = w
        return carry

    lax.fori_loop(0, tw, body, 0, unroll=64)
    for i in range(n):
        w_ref[:, i * n:(i + 1) * n] = wt_scr[pl.ds(i, tw, stride=W_ROW_PITCH), :].astype(BF16)


def _peer_wbuild(ei, ej, g):
    nsel, t = ei.shape
    tw = 128
    in_spec = pl.BlockSpec((nsel, tw), lambda i: (0, i))
    n = PEER_NKEYS
    return pl.pallas_call(
        _peer_wbuild_kernel, grid=(t // tw,), in_specs=[in_spec] * 3,
        out_specs=pl.BlockSpec((tw, n * n), lambda i: (i, 0)),
        out_shape=jax.ShapeDtypeStruct((t, n * n), BF16),
        scratch_shapes=[pltpu.VMEM((tw, nsel), F32)] * 3 + [pltpu.VMEM((tw * W_ROW_PITCH, n), F32)],
        compiler_params=_params(32), name="peer_wbuild")(ei, ej, g)


def _gelu_tanh(x):
    return x * (0.5 * (1.0 + jnp.tanh(math.sqrt(2.0 / math.pi) * (x + 0.044715 * (x * x * x)))))


def _peer_act_kernel(x_ref, u_ref, w_ref, p_ref):
    act = lax.dot_general(x_ref[...], u_ref[...].astype(BF16), _NT, preferred_element_type=F32)
    p_ref[...] = (w_ref[...].astype(F32) * _gelu_tanh(act)).astype(BF16)


def _peer_out_kernel(p_ref, v_ref, o_ref):
    e = pl.program_id(1)

    @pl.when(e == 0)
    def _():
        o_ref[...] = jnp.zeros_like(o_ref)

    o_ref[...] = o_ref[...] + jnp.dot(p_ref[...], v_ref[...].astype(BF16), preferred_element_type=F32)


def _peer_mlp(n2, u, v, w):
    t, d = n2.shape
    ne = u.shape[0]
    tm = _pick(t, (832, 640, 416, 320, 256, 128, 64, 8))
    eb = 512
    p = pl.pallas_call(
        _peer_act_kernel, grid=(ne // eb, t // tm),
        in_specs=[pl.BlockSpec((tm, d), lambda e, i: (i, 0)),
                  pl.BlockSpec((eb, d), lambda e, i: (e, 0)),
                  pl.BlockSpec((tm, eb), lambda e, i: (i, e))],
        out_specs=pl.BlockSpec((tm, eb), lambda e, i: (i, e)),
        out_shape=jax.ShapeDtypeStruct((t, ne), BF16),
        compiler_params=_params(56), name="peer_act")(n2, u, w)
    return pl.pallas_call(
        _peer_out_kernel, grid=(t // tm, ne // eb),
        in_specs=[pl.BlockSpec((tm, eb), lambda i, e: (i, e)),
                  pl.BlockSpec((eb, d), lambda i, e: (e, 0))],
        out_specs=pl.BlockSpec((tm, d), lambda i, e: (i, 0)),
        out_shape=jax.ShapeDtypeStruct((t, d), F32),
        compiler_params=_params(60), name="peer_out")(p, v)


def _pad_sample_rows(a, ns, ts):
    a = a.reshape(ns, ts, a.shape[-1])
    return jnp.pad(a, ((0, 0), (0, SAMPLE_PAD - ts), (0, 0)))


def kernel(x_prompt, x_sample, mem_prompt, cache_ret_state, cache_win_k1, cache_win_v1, cache_win_k2, cache_win_v2, cache_win_k3, cache_win_v3, cache_mem_k, cache_mem_v, norm1_w, w_in, ret_gn_w, w_ret_o, w_dil_o, mem_norm_w, w_mem_kv, w_mem_o, w_out, norm2_w, w_peer_q, peer_sub_keys, peer_u, peer_v, final_norm_w):
    batch, seq, d = x_prompt.shape
    ns, ts, _ = x_sample.shape
    mem_len = mem_prompt.shape[1]
    depth = norm1_w.shape[0]
    assert d == D_MODEL and seq % RET_CHUNK == 0 and ts <= SAMPLE_PAD
    tp, tsamp = batch * seq, ns * ts
    win_k = (cache_win_k1, cache_win_k2, cache_win_k3)
    win_v = (cache_win_v1, cache_win_v2, cache_win_v3)

    slopes = jnp.asarray([_alibi_slope(h) for h in range(DIL_HEADS)], F32)
    x_p, x_s = x_prompt.reshape(tp, d), x_sample.reshape(tsamp, d)
    prompt_states, sample_states = [], []
    for l in range(depth):
        proj = _matmul(_rmsnorm_pair(x_p, x_s, norm1_w[l], BF16), w_in[l],
                       tm_cands=(1664, 832, 640, 512, 416, 320, 256, 128, 64, 8), name="input_projection")
        proj_s = proj[tp:]

        ar_p, ret_p = _retention_prompt(proj, ret_gn_w[l], batch, seq)
        ar_s, ret_s = _retention_sample(proj_s, ret_gn_w[l], cache_ret_state[l], ts)

        outs_p, lses_p, outs_s, lses_s, bufs_p, bufs_s = [], [], [], [], [], []
        for g, (win, dil) in enumerate(DIL_GROUPS):
            o, lse = _dilated_prompt(proj, slopes, g, batch, seq)
            outs_p.append(o)
            lses_p.append(lse)
            keep = min(win, seq)
            for off in (OFF_DK, OFF_DV):
                c0 = off + g * DIL_W
                tail = [proj[b * seq + seq - keep:(b + 1) * seq, c0:c0 + DIL_W] for b in range(batch)]
                bufs_p.append(jnp.stack(tail, axis=0).reshape(batch, keep, DIL_HPG, DIL_HD))
            q_s, k_s, v_s = (proj_s[:, off + g * DIL_W: off + (g + 1) * DIL_W] for off in (OFF_DQ, OFF_DK, OFF_DV))
            o8, lse8, k_new, v_new = _dilated_sample(
                _pad_sample_rows(q_s, ns, ts), k_s.reshape(ns, ts * DIL_HPG, DIL_HD), v_s.reshape(ns, ts * DIL_HPG, DIL_HD),
                win_k[g], win_v[g], l, g, ts)
            outs_s.append(o8[:, :ts].reshape(tsamp, DIL_W))
            lses_s.append(lse8[:, :ts].reshape(tsamp, DIL_W))
            bufs_s += [k_new, v_new]
        ad_p = _combine_groups(outs_p, lses_p)
        ad_s = _combine_groups(outs_s, lses_s)

        mem_n = _rmsnorm(mem_prompt.reshape(batch * mem_len, d), mem_norm_w[l], BF16)
        mkv = _matmul(mem_n, w_mem_kv[l].astype(BF16), name="memory_kv").reshape(batch, mem_len, 2 * MEM_W)
        am_p = _memory_attend_prompt(proj, mkv, batch, seq)
        mq8 = _pad_sample_rows(proj_s[:, OFF_MQ:OFF_MQ + MEM_W], ns, ts)
        am_s = _memory_attend_sample(mq8, cache_mem_k, cache_mem_v, l)
        am_s = am_s[:, :ts].reshape(tsamp, MEM_W).astype(BF16)

        branch_w = (w_ret_o[l].astype(BF16), w_dil_o[l].astype(BF16), w_mem_o[l].astype(BF16))
        w_out_b = w_out[l].astype(BF16)
        h_p = _matmul(_branch_mix(ar_p, ad_p, am_p, *branch_w, proj, 0), w_out_b, res=x_p, name="output_projection")
        h_s = _matmul(_branch_mix(ar_s, ad_s, am_s, *branch_w, proj, tp), w_out_b, res=x_s, name="output_projection")

        n2 = _rmsnorm_pair(h_p, h_s, norm2_w[l], BF16)
        pq = _matmul(n2, w_peer_q[l].astype(BF16), name="peer_query")
        ei, ej, gw = _peer_route(pq, peer_sub_keys[l])
        w_dense = _peer_wbuild(ei, ej, gw)
        peer = _peer_mlp(n2, peer_u[l], peer_v[l], w_dense)

        mk = mkv[:, :, :MEM_W].reshape(batch, mem_len, MEM_HEADS, MEM_HD)
        mv = mkv[:, :, MEM_W:].reshape(batch, mem_len, MEM_HEADS, MEM_HD)
        prompt_states.append((ret_p, *bufs_p, mk, mv))
        sample_states.append((ret_s, *bufs_s))
        if l + 1 < depth:
            x_p, x_s = _add(h_p, peer, 0), _add(h_s, peer, tp)

    y_prompt = _add_rmsnorm(h_p, peer, final_norm_w, 0).reshape(batch, seq, d)
    y_sample = _add_rmsnorm(h_s, peer, final_norm_w, tp).reshape(ns, ts, d)
    p_out = [jnp.stack(s, axis=0) for s in zip(*prompt_states)]
    s_out = [jnp.stack(s, axis=0) for s in zip(*sample_states)]
    return (y_prompt, y_sample, *p_out, *s_out)
```
